```python
import math
import jax
import jax.numpy as jnp
from jax import lax
import numpy as np

D_MODEL = 1024
BATCH = 8
SEQ = 2048
DEPTH = 4

GLA_HEADS = 4
GLA_DK = 128
GLA_DV = 256
GLA_RANK = 16
GLA_TAU = 16.0
GLA_CHUNK = 64
MOBA_HEADS = 8
MOBA_DH = 64
MOBA_BLOCK = 256
MOBA_TOPK = 3
MOBA_Q_CHUNK = 16
REL_BUCKETS = 32
REL_MAX_DIST = 128
D_FF = 2816
CONV_W = 3
EPS = 1e-6
N_BRANCHES = 2
NEG_INF = -1e30

GLA_QK_W = GLA_HEADS * GLA_DK
GLA_V_W = GLA_HEADS * GLA_DV
MOBA_W = MOBA_HEADS * MOBA_DH
IN_SPLITS = (GLA_QK_W, GLA_QK_W, GLA_V_W, GLA_V_W, GLA_RANK, MOBA_W, MOBA_W, MOBA_W, N_BRANCHES * D_MODEL)
D_IN = 2 * GLA_QK_W + 2 * GLA_V_W + GLA_RANK + 3 * MOBA_W + N_BRANCHES * D_MODEL

kernel_name = 'gla_moba_gated_hybrid'


def rmsnorm(x, g):
    xf = x.astype(jnp.float32)
    y = xf * lax.rsqrt(jnp.mean(xf * xf, axis=-1, keepdims=True) + EPS)
    return (y * g.astype(jnp.float32)).astype(x.dtype)


def gla_chunked(q, k, v, log_a):
    B, T, H, DK = q.shape
    DV = v.shape[-1]
    C = GLA_CHUNK
    NC = T // C

    def to_chunks(t):
        return t.astype(jnp.float32).reshape(B, NC, C, H, t.shape[-1]).transpose(0, 3, 1, 2, 4)

    q, k, v, log_a = map(to_chunks, (q, k, v, log_a))
    q = q * (DK ** -0.5)
    b = jnp.cumsum(log_a, axis=3)
    b_last = b[:, :, :, -1:, :]
    q_dec = q * jnp.exp(b)
    k_dec = k * jnp.exp(-b)
    k_to_end = k * jnp.exp(b_last - b)
    causal = jnp.tril(jnp.ones((C, C), dtype=bool))
    attn = jnp.where(causal, jnp.einsum('bhnid,bhnjd->bhnij', q_dec, k_dec), 0.0)
    o_intra = jnp.einsum('bhnij,bhnjv->bhniv', attn, v)

    def step(S, inp):
        qc, kc, vc, dec = inp
        o = jnp.einsum('bhid,bhdv->bhiv', qc, S)
        S = S * dec[..., None] + jnp.einsum('bhjd,bhjv->bhdv', kc, vc)
        return S, o

    xs = (jnp.moveaxis(q_dec, 2, 0), jnp.moveaxis(k_to_end, 2, 0), jnp.moveaxis(v, 2, 0),
          jnp.moveaxis(jnp.exp(b_last[:, :, :, 0, :]), 2, 0))
    S0 = jnp.zeros((B, H, DK, DV), jnp.float32)
    _, o_inter = lax.scan(step, S0, xs)
    o = o_intra + jnp.moveaxis(o_inter, 0, 2)
    return o.transpose(0, 2, 3, 1, 4).reshape(B, T, H, DV)


def t5_bucket(rel):
    n = jnp.maximum(rel, 0)
    max_exact = REL_BUCKETS // 2
    nf = jnp.maximum(n, 1).astype(jnp.float32)
    large = max_exact + (jnp.log(nf / max_exact) / math.log(REL_MAX_DIST / max_exact)
                         * (REL_BUCKETS - max_exact)).astype(jnp.int32)
    large = jnp.minimum(large, REL_BUCKETS - 1)
    return jnp.where(n < max_exact, n, large)


def moba_attention(q, k, v, rel_bias):
    B, T, H, DH = q.shape
    NB = -(-T // MOBA_BLOCK)
    T_pad = NB * MOBA_BLOCK
    pad = ((0, 0), (0, T_pad - T), (0, 0), (0, 0))
    q, k, v = (jnp.pad(t, pad) for t in (q, k, v))
    QC = MOBA_Q_CHUNK
    NQ = T_pad // QC
    KSEL = min(MOBA_TOPK, NB)
    KB = KSEL * MOBA_BLOCK
    scale = DH ** -0.5
    k_blocks = k.transpose(0, 2, 1, 3).reshape(B, H, NB, MOBA_BLOCK, DH)
    v_blocks = v.transpose(0, 2, 1, 3).reshape(B, H, NB, MOBA_BLOCK, DH)
    k_mean = jnp.mean(k_blocks.astype(jnp.float32), axis=3)
    bias_tab = rel_bias.T.astype(jnp.float32)
    b_idx = jnp.arange(B)[:, None, None, None]
    h_idx = jnp.arange(H)[None, :, None, None]
    k_in_blk = jnp.arange(MOBA_BLOCK)
    q_chunks = q.transpose(0, 2, 1, 3).reshape(B, H, NQ, QC, DH).transpose(2, 0, 1, 3, 4)

    def one_chunk(args):
        qc, c = args
        q_pos = c * QC + jnp.arange(QC)
        q_blk = (c * QC) // MOBA_BLOCK
        gate = jnp.einsum('bhqd,bhnd->bhqn', qc.astype(jnp.float32), k_mean)
        gate = jnp.where(jnp.arange(NB) < q_blk, gate, -jnp.inf)
        _, sel = lax.top_k(gate, KSEL)
        sel_valid = jnp.repeat(sel < q_blk, MOBA_BLOCK, axis=-1)
        k_sel = k_blocks[b_idx, h_idx, sel].reshape(B, H, QC, KB, DH)
        v_sel = v_blocks[b_idx, h_idx, sel].reshape(B, H, QC, KB, DH)
        kpos_sel = (sel[..., None] * MOBA_BLOCK + k_in_blk).reshape(B, H, QC, KB)
        logit_sel = jnp.einsum('bhqd,bhqkd->bhqk', qc, k_sel).astype(jnp.float32) * scale
        logit_sel = logit_sel + bias_tab[h_idx, t5_bucket(q_pos[:, None] - kpos_sel)]
        logit_sel = jnp.where(sel_valid, logit_sel, NEG_INF)
        k_own = lax.dynamic_index_in_dim(k_blocks, q_blk, axis=2, keepdims=False)
        v_own = lax.dynamic_index_in_dim(v_blocks, q_blk, axis=2, keepdims=False)
        kpos_own = q_blk * MOBA_BLOCK + k_in_blk
        logit_own = jnp.einsum('bhqd,bhkd->bhqk', qc, k_own).astype(jnp.float32) * scale
        logit_own = logit_own + bias_tab[:, t5_bucket(q_pos[:, None] - kpos_own[None, :])]
        logit_own = jnp.where(kpos_own[None, :] <= q_pos[:, None], logit_own, NEG_INF)
        p = jax.nn.softmax(jnp.concatenate([logit_sel, logit_own], axis=-1), axis=-1).astype(v.dtype)
        return (jnp.einsum('bhqk,bhqkd->bhqd', p[..., :KB], v_sel)
                + jnp.einsum('bhqk,bhkd->bhqd', p[..., KB:], v_own))

    o = lax.map(one_chunk, (q_chunks, jnp.arange(NQ)))
    o = o.transpose(1, 0, 3, 2, 4).reshape(B, T_pad, H * DH)
    return o[:, :T]


def causal_dwconv(u, w, b):
    out = lax.conv_general_dilated(u, w[:, None, :].astype(u.dtype), window_strides=(1,),
                                   padding=[(CONV_W - 1, 0)], dimension_numbers=('NWC', 'WIO', 'NWC'),
                                   feature_group_count=u.shape[-1])
    return out + b.astype(u.dtype)


def hybrid_layer(x, rel_bias, norm_mix, w_in, w_lr_up, b_forget, gla_out_norm, w_branch_gla,
                 w_branch_moba, w_out, norm_ffn, w_up, conv_w, conv_b, w_down):
    B, T, _ = x.shape
    h = rmsnorm(x, norm_mix)
    proj = h @ w_in
    points = [int(p) for p in np.cumsum(IN_SPLITS)[:-1]]
    qa, ka, va, ra, a_lr, qb, kb, vb, gates = jnp.split(proj, points, axis=-1)
    log_a = jax.nn.log_sigmoid((a_lr @ w_lr_up + b_forget).astype(jnp.float32)) / GLA_TAU
    oa = gla_chunked(qa.reshape(B, T, GLA_HEADS, GLA_DK), ka.reshape(B, T, GLA_HEADS, GLA_DK),
                     va.reshape(B, T, GLA_HEADS, GLA_DV), log_a.reshape(B, T, GLA_HEADS, GLA_DK))
    oa = oa * lax.rsqrt(jnp.mean(oa * oa, axis=-1, keepdims=True) + EPS)
    oa = (oa.reshape(B, T, GLA_V_W) * gla_out_norm.astype(jnp.float32)).astype(x.dtype) * jax.nn.silu(ra)
    ob = moba_attention(qb.reshape(B, T, MOBA_HEADS, MOBA_DH), kb.reshape(B, T, MOBA_HEADS, MOBA_DH),
                        vb.reshape(B, T, MOBA_HEADS, MOBA_DH), rel_bias)
    g_a, g_b = jnp.split(jax.nn.sigmoid(gates), 2, axis=-1)
    mixed = g_a * (oa @ w_branch_gla) + g_b * (ob @ w_branch_moba)
    x = x + mixed @ w_out
    h = rmsnorm(x, norm_ffn)
    u = causal_dwconv(h @ w_up, conv_w, conv_b)
    a, bval = jnp.split(u, 2, axis=-1)
    return x + (jax.nn.silu(a) * bval) @ w_down


def setup_inputs(seed: int = 0) -> dict:
    key = jax.random.key(seed)
    ks = jax.random.split(key, 16)

    def nrm(k, shape, scale):
        return jax.random.normal(k, shape, jnp.float32) * scale

    return {
        'x': nrm(ks[0], (BATCH, SEQ, D_MODEL), 1.0),
        'rel_bias': nrm(ks[1], (REL_BUCKETS, MOBA_HEADS), 0.5),
        'norm_mix': 1.0 + nrm(ks[2], (DEPTH, D_MODEL), 0.05),
        'w_in': nrm(ks[3], (DEPTH, D_MODEL, D_IN), D_MODEL ** -0.5),
        'w_lr_up': nrm(ks[4], (DEPTH, GLA_RANK, GLA_QK_W), GLA_RANK ** -0.5),
        'b_forget': 1.0 + nrm(ks[5], (DEPTH, GLA_QK_W), 0.5),
        'gla_out_norm': 1.0 + nrm(ks[6], (DEPTH, GLA_V_W), 0.05),
        'w_branch_gla': nrm(ks[7], (DEPTH, GLA_V_W, D_MODEL), GLA_V_W ** -0.5),
        'w_branch_moba': nrm(ks[8], (DEPTH, MOBA_W, D_MODEL), MOBA_W ** -0.5),
        'w_out': nrm(ks[9], (DEPTH, D_MODEL, D_MODEL), D_MODEL ** -0.5),
        'norm_ffn': 1.0 + nrm(ks[10], (DEPTH, D_MODEL), 0.05),
        'w_up': nrm(ks[11], (DEPTH, D_MODEL, 2 * D_FF), D_MODEL ** -0.5),
        'conv_w': nrm(ks[12], (DEPTH, CONV_W, 2 * D_FF), CONV_W ** -0.5),
        'conv_b': nrm(ks[13], (DEPTH, 2 * D_FF), 0.02),
        'w_down': nrm(ks[14], (DEPTH, D_FF, D_MODEL), D_FF ** -0.5),
        'norm_final': 1.0 + nrm(ks[15], (D_MODEL,), 0.05),
    }


def reference(x, rel_bias, norm_mix, w_in, w_lr_up, b_forget, gla_out_norm, w_branch_gla,
              w_branch_moba, w_out, norm_ffn, w_up, conv_w, conv_b, w_down, norm_final):
    for l in range(DEPTH):
        x = hybrid_layer(x, rel_bias, norm_mix[l], w_in[l], w_lr_up[l], b_forget[l], gla_out_norm[l],
                         w_branch_gla[l], w_branch_moba[l], w_out[l], norm_ffn[l], w_up[l], conv_w[l],
                         conv_b[l], w_down[l])
    return rmsnorm(x, norm_final)
```

```python
import functools
import math

import jax
import jax.numpy as jnp
import numpy as np
from jax import lax
from jax.experimental import pallas as pl
from jax.experimental.pallas import tpu as pltpu

GLA_HEADS = 4
GLA_DK = 128
GLA_DV = 256
GLA_RANK = 16
GLA_TAU = 16.0
GLA_CHUNK = 64
MOBA_HEADS = 8
MOBA_DH = 64
MOBA_BLOCK = 256
MOBA_TOPK = 3
REL_BUCKETS = 32
REL_MAX_DIST = 128
CONV_W = 3
EPS = 1e-6
NEG_INF = -1e30

GLA_QK_W = GLA_HEADS * GLA_DK
GLA_V_W = GLA_HEADS * GLA_DV
MOBA_W = MOBA_HEADS * MOBA_DH

LANES = 128
VMEM_LIMIT_BYTES = 56 * 1024 * 1024

F32 = jnp.float32
BF16 = jnp.bfloat16
HIGHEST = lax.Precision.HIGHEST

_NT = (((1,), (1,)), ((), ()))
_TN = (((0,), (0,)), ((), ()))


def _dot(a, b, precision=None):
    return jnp.dot(a, b, preferred_element_type=F32, precision=precision)


def _dot_nt(a, b, precision=None):
    return lax.dot_general(a, b, _NT, preferred_element_type=F32, precision=precision)


def _dot_tn(a, b):
    return lax.dot_general(a, b, _TN, preferred_element_type=F32)


def _sigmoid(x):
    return 1.0 / (1.0 + jnp.exp(-x))


def _params(*semantics):
    return pltpu.CompilerParams(dimension_semantics=semantics, vmem_limit_bytes=VMEM_LIMIT_BYTES)


def _resident(shape, index_map):
    return pl.BlockSpec(shape, index_map, pipeline_mode=pl.Buffered(1))


D_MERGE = 2
ALR_PAD = LANES


def _proj_layout(d_model):
    f32_cols = dict(gates=(0, D_MERGE * d_model))
    off = D_MERGE * d_model
    for name, w in (("ra", GLA_V_W), ("qa", GLA_QK_W), ("ka", GLA_QK_W), ("alr", ALR_PAD)):
        f32_cols[name] = (off, w)
        off += w
    wf = off
    bf_cols = {}
    off = 0
    for name, w in (("va", GLA_V_W), ("qb", MOBA_W), ("kb", MOBA_W), ("vb", MOBA_W)):
        bf_cols[name] = (off, w)
        off += w
    return f32_cols, wf, bf_cols, off


def _regroup_w_in(w_in, d_model):
    splits = (GLA_QK_W, GLA_QK_W, GLA_V_W, GLA_V_W, GLA_RANK, MOBA_W, MOBA_W, MOBA_W, D_MERGE * d_model)
    pts = np.cumsum(splits)[:-1].tolist()
    qa, ka, va, ra, alr, qb, kb, vb, gates = jnp.split(w_in, pts, axis=-1)
    alr = jnp.pad(alr, ((0, 0), (0, 0), (0, ALR_PAD - GLA_RANK)))
    return jnp.concatenate([gates, ra, qa, ka, alr, va, qb, kb, vb], axis=-1).astype(BF16)


def _in_proj_kernel(x_ref, g_ref, w_ref, of_ref, ob_ref, *, wf, wb, chunk):
    x = x_ref[...]
    ms = jnp.mean(x * x, axis=-1, keepdims=True)
    h = (x * lax.rsqrt(ms + EPS) * g_ref[...]).astype(BF16)
    for c0 in range(0, wf, chunk):
        c1 = min(c0 + chunk, wf)
        of_ref[:, c0:c1] = _dot(h, w_ref[:, c0:c1])
    for c0 in range(0, wb, chunk):
        c1 = min(c0 + chunk, wb)
        ob_ref[:, c0:c1] = _dot(h, w_ref[:, wf + c0:wf + c1]).astype(BF16)


def _in_proj(x2, g, w, layer, *, wf, wb, tm):
    n, d = x2.shape
    kern = functools.partial(_in_proj_kernel, wf=wf, wb=wb, chunk=512)
    return pl.pallas_call(
        kern,
        grid=(n // tm,),
        in_specs=[
            pl.BlockSpec((tm, d), lambda i: (i, 0)),
            pl.BlockSpec((None, 1, d), lambda i: (layer, 0, 0)),
            _resident((None, d, wf + wb), lambda i: (layer, 0, 0)),
        ],
        out_specs=[
            pl.BlockSpec((tm, wf), lambda i: (i, 0)),
            pl.BlockSpec((tm, wb), lambda i: (i, 0)),
        ],
        out_shape=[jax.ShapeDtypeStruct((n, wf), F32), jax.ShapeDtypeStruct((n, wb), BF16)],
        compiler_params=_params("parallel"),
        name="in_proj",
    )(x2, g, w)


def _gla_kernel(q_ref, k_ref, v_ref, r_ref, alr_ref, wlr_ref, bf_ref, gn_ref, o_ref, st_ref, *, seq):
    c = GLA_CHUNK
    st_ref[...] = jnp.zeros_like(st_ref)
    row = lax.broadcasted_iota(jnp.int32, (c, c), 0)
    col = lax.broadcasted_iota(jnp.int32, (c, c), 1)
    causal = row >= col
    tril = causal.astype(F32)
    wlr = wlr_ref[...]
    bfg = bf_ref[...]
    gn = gn_ref[...]
    q_scale = GLA_DK ** -0.5

    def body(n, carry):
        r0 = pl.multiple_of(n * c, c)
        q = q_ref[pl.ds(r0, c), :]
        k = k_ref[pl.ds(r0, c), :]
        v = v_ref[pl.ds(r0, c), :]
        xa = _dot(alr_ref[pl.ds(r0, c), :], wlr, HIGHEST) + bfg
        log_a = (jnp.minimum(xa, 0.0) - jnp.log1p(jnp.exp(-jnp.abs(xa)))) * (1.0 / GLA_TAU)
        b = _dot(tril, log_a, HIGHEST)
        b_last = b[c - 1:c, :]
        q_dec = ((q * q_scale) * jnp.exp(b)).astype(BF16)
        k_dec = (k * jnp.exp(-b)).astype(BF16)
        k_end = (k * jnp.exp(b_last - b)).astype(BF16)
        attn = jnp.where(causal, _dot_nt(q_dec, k_dec), 0.0).astype(BF16)
        st = st_ref[...]
        o = _dot(attn, v) + _dot_nt(q_dec, st.astype(BF16))
        st_ref[...] = st * jnp.exp(b_last) + _dot_tn(v, k_end)
        ms = jnp.mean(o * o, axis=-1, keepdims=True)
        r = r_ref[pl.ds(r0, c), :]
        out = (o * lax.rsqrt(ms + EPS) * gn) * (r * _sigmoid(r))
        o_ref[pl.ds(r0, c), :] = out.astype(BF16)
        return carry

    lax.fori_loop(0, seq // c, body, 0)


def _gla(pf3, pb3, wlr, bforget, gnorm, layer, *, fcols, bcols):
    bsz, seq, _ = pf3.shape
    qa0, ka0, ra0, alr0 = (fcols[k][0] for k in ("qa", "ka", "ra", "alr"))
    va0 = bcols["va"][0]
    kern = functools.partial(_gla_kernel, seq=seq)
    return pl.pallas_call(
        kern,
        grid=(bsz, GLA_HEADS),
        in_specs=[
            pl.BlockSpec((None, seq, GLA_DK), lambda b, h: (b, 0, qa0 // GLA_DK + h)),
            pl.BlockSpec((None, seq, GLA_DK), lambda b, h: (b, 0, ka0 // GLA_DK + h)),
            pl.BlockSpec((None, seq, GLA_DV), lambda b, h: (b, 0, va0 // GLA_DV + h)),
            pl.BlockSpec((None, seq, GLA_DV), lambda b, h: (b, 0, ra0 // GLA_DV + h)),
            pl.BlockSpec((None, seq, ALR_PAD), lambda b, h: (b, 0, alr0 // ALR_PAD)),
            pl.BlockSpec((None, ALR_PAD, GLA_DK), lambda b, h: (layer, 0, h)),
            pl.BlockSpec((None, 1, GLA_DK), lambda b, h: (layer, 0, h)),
            pl.BlockSpec((None, 1, GLA_DV), lambda b, h: (layer, 0, h)),
        ],
        out_specs=pl.BlockSpec((None, seq, GLA_DV), lambda b, h: (b, 0, h)),
        out_shape=jax.ShapeDtypeStruct((bsz, seq, GLA_V_W), BF16),
        scratch_shapes=[pltpu.VMEM((GLA_DV, GLA_DK), F32)],
        compiler_params=_params("parallel", "parallel"),
        name="gla",
    )(pf3, pf3, pb3, pf3, pf3, wlr, bforget, gnorm)


def _t5_bucket(rel):
    n = jnp.maximum(rel, 0)
    max_exact = REL_BUCKETS // 2
    nf = jnp.maximum(n, 1).astype(F32)
    large = max_exact + (jnp.log(nf / max_exact) / math.log(REL_MAX_DIST / max_exact)
                         * (REL_BUCKETS - max_exact)).astype(jnp.int32)
    large = jnp.minimum(large, REL_BUCKETS - 1)
    return jnp.where(n < max_exact, n, large)


def _far_bucket(seq):
    d = np.arange(MOBA_BLOCK + 1, max(seq, MOBA_BLOCK + 2), dtype=np.float64)
    max_exact = REL_BUCKETS // 2
    large = max_exact + np.floor(np.log(d / max_exact) / math.log(REL_MAX_DIST / max_exact)
                                 * (REL_BUCKETS - max_exact) + 1e-6)
    assert MOBA_BLOCK + 1 >= max_exact and np.all(large >= REL_BUCKETS - 1), "far blocks must share one bucket"
    return REL_BUCKETS - 1


def _bias_kernel(bucket_ref, rb_ref, o_ref, *, far_bucket):
    h = pl.program_id(0)
    blk = MOBA_BLOCK
    far = rb_ref[far_bucket, h]
    row = lax.broadcasted_iota(jnp.int32, (blk, blk), 0)
    col = lax.broadcasted_iota(jnp.int32, (blk, blk), 1)
    o_ref[0] = jnp.zeros((blk, blk), F32)
    for t in range(2):
        bk = bucket_ref[t]
        acc = jnp.zeros((blk, blk), F32)
        for b in range(REL_BUCKETS):
            acc = jnp.where(bk == b, rb_ref[b, h], acc)
        acc = acc - far
        if t == 1:
            acc = jnp.where(row >= col, acc, NEG_INF)
        o_ref[t + 1] = acc


def _bias_tiles(rel_bias, seq):
    blk = MOBA_BLOCK
    r = jnp.arange(blk, dtype=jnp.int32)[:, None]
    c = jnp.arange(blk, dtype=jnp.int32)[None, :]
    bucket = jnp.stack([_t5_bucket(r + blk - c), _t5_bucket(r - c)])
    kern = functools.partial(_bias_kernel, far_bucket=_far_bucket(seq))
    return pl.pallas_call(
        kern,
        grid=(MOBA_HEADS,),
        in_specs=[
            pl.BlockSpec((2, blk, blk), lambda h: (0, 0, 0)),
            pl.BlockSpec(memory_space=pltpu.SMEM),
        ],
        out_specs=pl.BlockSpec((None, 3, blk, blk), lambda h: (h, 0, 0, 0)),
        out_shape=jax.ShapeDtypeStruct((MOBA_HEADS, 3, blk, blk), F32),
        compiler_params=_params("arbitrary"),
        name="rel_bias_tiles",
    )(bucket, rel_bias)


def _moba_kernel(q_ref, k_ref, v_ref, bias_ref, o_ref, kaug_ref, kmean_ref, *, nb):
    i = pl.program_id(1)
    blk, dh, heads = MOBA_BLOCK, MOBA_DH, MOBA_HEADS
    seq = nb * blk
    pair_w = 2 * dh
    assert pair_w == LANES and nb <= dh
    lane = lax.broadcasted_iota(jnp.int32, (1, pair_w), 1)

    @pl.when(i == 0)
    def _build_keys():
        rowblk = lax.broadcasted_iota(jnp.int32, (seq, pair_w), 0) // blk
        lane_s = lax.broadcasted_iota(jnp.int32, (seq, pair_w), 1)
        for p in range(heads // 2):
            kp = k_ref[:, p * pair_w:(p + 1) * pair_w]
            km = jnp.sum(kp.astype(F32).reshape(nb, blk, pair_w), axis=1) * (1.0 / blk)
            kmean_ref[p] = km
            for e in range(2):
                own = (lane_s >= dh) if e else (lane_s < dh)
                onehot = (lane_s - (0 if e else dh)) == rowblk
                kaug_ref[2 * p + e] = jnp.where(own, kp, onehot.astype(BF16))

    eye = (lax.broadcasted_iota(jnp.int32, (blk, blk), 0)
           == lax.broadcasted_iota(jnp.int32, (blk, blk), 1)).astype(BF16)
    row8 = lax.broadcasted_iota(jnp.int32, (nb, blk), 0)
    scale = dh ** -0.5

    for p in range(heads // 2):
        qp = q_ref[:, p * pair_w:(p + 1) * pair_w]
        km = kmean_ref[p]
        outs = []
        for e in range(2):
            own = (lane >= dh) if e else (lane < dh)
            kme = jnp.where(own, km, 0.0)
            km_hi = kme.astype(BF16)
            km_lo = (kme - km_hi.astype(F32)).astype(BF16)
            g = _dot_nt(km_hi, qp) + _dot_nt(km_lo, qp)
            g = jnp.where(row8 < i, g, -jnp.inf)
            rank = jnp.zeros((nb, blk), jnp.int32)
            for m in range(nb):
                gm = g[m:m + 1, :]
                beats = (gm > g) | ((gm == g) & (m < row8))
                rank = rank + beats.astype(jnp.int32)
            keep = ((rank < MOBA_TOPK) & (row8 < i)) | (row8 == i)
            pen_t = jnp.where(keep, 0.0, NEG_INF)
            pad_lo = dh if e == 0 else 0
            parts = []
            if pad_lo:
                parts.append(jnp.zeros((pad_lo, blk), F32))
            parts.append(pen_t)
            parts.append(jnp.zeros((pair_w - pad_lo - nb, blk), F32))
            pen_full = jnp.concatenate(parts, axis=0).astype(BF16)
            pen = _dot_nt(eye, pen_full)
            q_aug = jnp.where(own, qp.astype(F32) * scale, pen).astype(BF16)

            h = 2 * p + e

            def jbody(j, carry, h=h, q_aug=q_aug, p=p):
                m_i, l_i, acc = carry
                k0 = pl.multiple_of(j * blk, blk)
                s = _dot_nt(q_aug, kaug_ref[h, pl.ds(k0, blk), :])
                s = s + bias_ref[h, jnp.clip(j - i + 2, 0, 2)]
                m_new = jnp.maximum(m_i, jnp.max(s, axis=1, keepdims=True))
                alpha = jnp.exp(m_i - m_new)
                pr = jnp.exp(s - m_new)
                l_new = alpha * l_i + jnp.sum(pr, axis=1, keepdims=True)
                vj = v_ref[pl.ds(k0, blk), p * pair_w:(p + 1) * pair_w]
                acc = alpha * acc + _dot(pr.astype(BF16), vj)
                return m_new, l_new, acc

            init = (jnp.full((blk, 1), -jnp.inf, F32), jnp.zeros((blk, 1), F32), jnp.zeros((blk, pair_w), F32))
            _, l_f, acc_f = lax.fori_loop(0, i + 1, jbody, init)
            outs.append(acc_f / l_f)
        o_ref[:, p * pair_w:(p + 1) * pair_w] = jnp.where(lane < dh, outs[0], outs[1]).astype(BF16)


def _moba(pb3, bias, *, bcols):
    bsz, seq, _ = pb3.shape
    blk = MOBA_BLOCK
    nb = seq // blk
    qb0, kb0, vb0 = (bcols[k][0] for k in ("qb", "kb", "vb"))
    kern = functools.partial(_moba_kernel, nb=nb)
    return pl.pallas_call(
        kern,
        grid=(bsz, nb),
        in_specs=[
            pl.BlockSpec((None, blk, MOBA_W), lambda b, i: (b, i, qb0 // MOBA_W)),
            pl.BlockSpec((None, seq, MOBA_W), lambda b, i: (b, 0, kb0 // MOBA_W)),
            pl.BlockSpec((None, seq, MOBA_W), lambda b, i: (b, 0, vb0 // MOBA_W)),
            _resident((MOBA_HEADS, 3, blk, blk), lambda b, i: (0, 0, 0, 0)),
        ],
        out_specs=pl.BlockSpec((None, blk, MOBA_W), lambda b, i: (b, i, 0)),
        out_shape=jax.ShapeDtypeStruct((bsz, seq, MOBA_W), BF16),
        scratch_shapes=[
            pltpu.VMEM((MOBA_HEADS, seq, 2 * MOBA_DH), BF16),
            pltpu.VMEM((MOBA_HEADS // 2, nb, 2 * MOBA_DH), F32),
        ],
        compiler_params=_params("arbitrary", "arbitrary"),
        name="moba",
    )(pb3, pb3, pb3, bias)


def _merge_kernel(oa_ref, ob_ref, g_ref, x_ref, wg_ref, wm_ref, wo_ref, o_ref, *, d):
    ya = _dot(oa_ref[...], wg_ref[...])
    yb = _dot(ob_ref[...], wm_ref[...])
    mixed = _sigmoid(g_ref[:, 0:d]) * ya + _sigmoid(g_ref[:, d:2 * d]) * yb
    o_ref[...] = x_ref[...] + _dot(mixed.astype(BF16), wo_ref[...])


def _merge(oa2, ob2, pf2, x2, wg, wm, wo, layer, *, tm):
    n, d = x2.shape
    kern = functools.partial(_merge_kernel, d=d)
    return pl.pallas_call(
        kern,
        grid=(n // tm,),
        in_specs=[
            pl.BlockSpec((tm, GLA_V_W), lambda i: (i, 0)),
            pl.BlockSpec((tm, MOBA_W), lambda i: (i, 0)),
            pl.BlockSpec((tm, D_MERGE * d), lambda i: (i, 0)),
            pl.BlockSpec((tm, d), lambda i: (i, 0)),
            _resident((None, GLA_V_W, d), lambda i: (layer, 0, 0)),
            _resident((None, MOBA_W, d), lambda i: (layer, 0, 0)),
            _resident((None, d, d), lambda i: (layer, 0, 0)),
        ],
        out_specs=pl.BlockSpec((tm, d), lambda i: (i, 0)),
        out_shape=jax.ShapeDtypeStruct((n, d), F32),
        compiler_params=_params("parallel"),
        name="merge_out",
    )(oa2, ob2, pf2, x2, wg, wm, wo)


HALO = 8


def _ffn_kernel(x_ref, g_ref, wu_ref, cw_ref, cb_ref, wd_ref, gf_ref, o_ref, ubuf_ref, carry_ref,
                *, d_ff, cw, final):
    t = pl.program_id(1)
    tm = x_ref.shape[0]

    @pl.when(t == 0)
    def _zero_history():
        carry_ref[...] = jnp.zeros_like(carry_ref)

    x = x_ref[...]
    ms = jnp.mean(x * x, axis=-1, keepdims=True)
    h = (x * lax.rsqrt(ms + EPS) * g_ref[...]).astype(BF16)
    acc = jnp.zeros((tm, x.shape[1]), F32)
    for c in range(d_ff // cw):
        halves = []
        for part in range(2):
            c0 = part * d_ff + c * cw
            u = _dot(h, wu_ref[:, c0:c0 + cw])
            ubuf_ref[0:HALO, :] = carry_ref[:, c0:c0 + cw]
            ubuf_ref[HALO:HALO + tm, :] = u
            carry_ref[:, c0:c0 + cw] = u[tm - HALO:tm, :]
            w = cw_ref[:, c0:c0 + cw]
            conv = cb_ref[:, c0:c0 + cw] + w[CONV_W - 1:CONV_W, :] * u
            for s in range(1, CONV_W):
                conv = conv + w[CONV_W - 1 - s:CONV_W - s, :] * ubuf_ref[HALO - s:HALO - s + tm, :]
            halves.append(conv)
        a, bval = halves
        act = ((a * _sigmoid(a)) * bval).astype(BF16)
        acc = acc + _dot(act, wd_ref[c * cw:(c + 1) * cw, :])
    y = x + acc
    if final:
        ms2 = jnp.mean(y * y, axis=-1, keepdims=True)
        y = y * lax.rsqrt(ms2 + EPS) * gf_ref[...]
    o_ref[...] = y


def _ffn(x3, g, wu, cw_, cb, wd, gfinal, layer, *, tm, final):
    bsz, seq, d = x3.shape
    d_ff = wd.shape[1]
    cw = 256
    assert d_ff % cw == 0 and seq % tm == 0
    kern = functools.partial(_ffn_kernel, d_ff=d_ff, cw=cw, final=final)
    return pl.pallas_call(
        kern,
        grid=(bsz, seq // tm),
        in_specs=[
            pl.BlockSpec((None, tm, d), lambda b, t: (b, t, 0)),
            pl.BlockSpec((None, 1, d), lambda b, t: (layer, 0, 0)),
            _resident((None, d, 2 * d_ff), lambda b, t: (layer, 0, 0)),
            pl.BlockSpec((None, CONV_W, 2 * d_ff), lambda b, t: (layer, 0, 0)),
            pl.BlockSpec((None, 1, 2 * d_ff), lambda b, t: (layer, 0, 0)),
            _resident((None, d_ff, d), lambda b, t: (layer, 0, 0)),
            pl.BlockSpec((1, d), lambda b, t: (0, 0)),
        ],
        out_specs=pl.BlockSpec((None, tm, d), lambda b, t: (b, t, 0)),
        out_shape=jax.ShapeDtypeStruct((bsz, seq, d), F32),
        scratch_shapes=[
            pltpu.VMEM((HALO + tm, cw), F32),
            pltpu.VMEM((HALO, 2 * d_ff), F32),
        ],
        compiler_params=_params("arbitrary", "arbitrary"),
        name="ffn",
    )(x3, g, wu, cw_, cb, wd, gfinal)


def kernel(x, rel_bias, norm_mix, w_in, w_lr_up, b_forget, gla_out_norm, w_branch_gla, w_branch_moba,
           w_out, norm_ffn, w_up, conv_w, conv_b, w_down, norm_final):
    bsz, seq, d = x.shape
    depth = w_in.shape[0]
    n = bsz * seq
    fcols, wf, bcols, wb = _proj_layout(d)

    w_in_r = _regroup_w_in(w_in, d)
    wlr = jnp.pad(w_lr_up, ((0, 0), (0, ALR_PAD - GLA_RANK), (0, 0)))
    wg = w_branch_gla.astype(BF16)
    wm = w_branch_moba.astype(BF16)
    wo = w_out.astype(BF16)
    wu = w_up.astype(BF16)
    wd = w_down.astype(BF16)
    norm_mix3 = norm_mix[:, None, :]
    norm_ffn3 = norm_ffn[:, None, :]
    b_forget3 = b_forget[:, None, :]
    gnorm3 = gla_out_norm[:, None, :]
    conv_b3 = conv_b[:, None, :]
    gfinal = norm_final[None, :]

    bias = _bias_tiles(rel_bias, seq)

    for l in range(depth):
        pf, pb = _in_proj(x.reshape(n, d), norm_mix3, w_in_r, l, wf=wf, wb=wb, tm=512)
        pf3 = pf.reshape(bsz, seq, wf)
        pb3 = pb.reshape(bsz, seq, wb)
        oa = _gla(pf3, pb3, wlr, b_forget3, gnorm3, l, fcols=fcols, bcols=bcols)
        ob = _moba(pb3, bias, bcols=bcols)
        x1 = _merge(oa.reshape(n, GLA_V_W), ob.reshape(n, MOBA_W), pf, x.reshape(n, d), wg, wm, wo, l, tm=512)
        x = _ffn(x1.reshape(bsz, seq, d), norm_ffn3, wu, conv_w, conv_b3, wd, gfinal, l,
                 tm=512, final=(l == depth - 1))
    return x
```

```python
import functools
import math

import jax
import jax.numpy as jnp
import numpy as np
from jax import lax
from jax.experimental import pallas as pl
from jax.experimental.pallas import tpu as pltpu

GLA_HEADS = 4
GLA_DK = 128
GLA_DV = 256
GLA_RANK = 16
GLA_TAU = 16.0
GLA_CHUNK = 64
MOBA_HEADS = 8
MOBA_DH = 64
MOBA_BLOCK = 256
MOBA_TOPK = 3
REL_BUCKETS = 32
REL_MAX_DIST = 128
CONV_W = 3
EPS = 1e-6
NEG_INF = -1e30

GLA_QK_W = GLA_HEADS * GLA_DK
GLA_V_W = GLA_HEADS * GLA_DV
MOBA_W = MOBA_HEADS * MOBA_DH

LANES = 128
VMEM_LIMIT_BYTES = 56 * 1024 * 1024

F32 = jnp.float32
BF16 = jnp.bfloat16
HIGHEST = lax.Precision.HIGHEST

_NT = (((1,), (1,)), ((), ()))
_TN = (((0,), (0,)), ((), ()))


def _dot(a, b, precision=None):
    return jnp.dot(a, b, preferred_element_type=F32, precision=precision)


def _dot_nt(a, b, precision=None):
    return lax.dot_general(a, b, _NT, preferred_element_type=F32, precision=precision)


def _dot_tn(a, b):
    return lax.dot_general(a, b, _TN, preferred_element_type=F32)


def _sigmoid(x):
    return 1.0 / (1.0 + jnp.exp(-x))


def _params(*semantics):
    return pltpu.CompilerParams(dimension_semantics=semantics, vmem_limit_bytes=VMEM_LIMIT_BYTES)


def _resident(shape, index_map):
    return pl.BlockSpec(shape, index_map, pipeline_mode=pl.Buffered(1))


D_MERGE = 2
ALR_PAD = LANES


def _proj_layout(d_model):
    f32_cols = dict(gates=(0, D_MERGE * d_model))
    off = D_MERGE * d_model
    for name, w in (("ra", GLA_V_W), ("qa", GLA_QK_W), ("ka", GLA_QK_W), ("alr", ALR_PAD)):
        f32_cols[name] = (off, w)
        off += w
    wf = off
    bf_cols = {}
    off = 0
    for name, w in (("va", GLA_V_W), ("kb", MOBA_W)):
        bf_cols[name] = (off, w)
        off += w
    return f32_cols, wf, bf_cols, off


def _regroup_w_in(w_in, d_model):
    splits = (GLA_QK_W, GLA_QK_W, GLA_V_W, GLA_V_W, GLA_RANK, MOBA_W, MOBA_W, MOBA_W, D_MERGE * d_model)
    pts = np.cumsum(splits)[:-1].tolist()
    qa, ka, va, ra, alr, qb, kb, vb, gates = jnp.split(w_in, pts, axis=-1)
    alr = jnp.pad(alr, ((0, 0), (0, 0), (0, ALR_PAD - GLA_RANK)))
    w_rows = jnp.concatenate([gates, ra, qa, ka, alr, va, kb], axis=-1).astype(BF16)
    w_t = jnp.swapaxes(jnp.concatenate([qb, vb], axis=-1), 1, 2).astype(BF16)
    return w_rows, w_t


def _in_proj_kernel(x_ref, g_ref, w_ref, wt_ref, of_ref, ob_ref, qt_ref, vt_ref, *, wf, wb, chunk):
    x = x_ref[...]
    ms = jnp.mean(x * x, axis=-1, keepdims=True)
    h = (x * lax.rsqrt(ms + EPS) * g_ref[...]).astype(BF16)
    for c0 in range(0, wf, chunk):
        c1 = min(c0 + chunk, wf)
        of_ref[:, c0:c1] = _dot(h, w_ref[:, c0:c1])
    for c0 in range(0, wb, chunk):
        c1 = min(c0 + chunk, wb)
        ob_ref[:, c0:c1] = _dot(h, w_ref[:, wf + c0:wf + c1]).astype(BF16)
    blk = MOBA_BLOCK
    for s in range(x.shape[0] // blk):
        hs = h[s * blk:(s + 1) * blk, :]
        qt_ref[s] = _dot_nt(wt_ref[0:MOBA_W, :], hs).astype(BF16)
        vt_ref[s] = _dot_nt(wt_ref[MOBA_W:2 * MOBA_W, :], hs).astype(BF16)


def _in_proj(x2, g, w, wt, layer, *, seq, wf, wb, tm):
    n, d = x2.shape
    blk = MOBA_BLOCK
    tpb = seq // tm
    spt = tm // blk
    kern = functools.partial(_in_proj_kernel, wf=wf, wb=wb, chunk=512)
    t_shape = jax.ShapeDtypeStruct((n // seq, seq // blk, MOBA_W, blk), BF16)
    t_spec = pl.BlockSpec((None, spt, MOBA_W, blk), lambda i: (i // tpb, i % tpb, 0, 0))
    return pl.pallas_call(
        kern,
        grid=(n // tm,),
        in_specs=[
            pl.BlockSpec((tm, d), lambda i: (i, 0)),
            pl.BlockSpec((None, 1, d), lambda i: (layer, 0, 0)),
            _resident((None, d, wf + wb), lambda i: (layer, 0, 0)),
            _resident((None, 2 * MOBA_W, d), lambda i: (layer, 0, 0)),
        ],
        out_specs=[
            pl.BlockSpec((tm, wf), lambda i: (i, 0)),
            pl.BlockSpec((tm, wb), lambda i: (i, 0)),
            t_spec,
            t_spec,
        ],
        out_shape=[jax.ShapeDtypeStruct((n, wf), F32), jax.ShapeDtypeStruct((n, wb), BF16), t_shape, t_shape],
        compiler_params=_params("parallel"),
        name="in_proj",
    )(x2, g, w, wt)


GLA_SUPER = 256


def _gla_kernel(q_ref, k_ref, v_ref, r_ref, alr_ref, wlr_ref, bf_ref, gn_ref, o_ref,
                qd_ref, u_ref, dec_ref, oi_ref, sb_ref, *, seq):
    c, sb = GLA_CHUNK, GLA_SUPER
    cps = sb // c
    nsb = seq // sb
    nchunks = seq // c
    shift = c.bit_length() - 1
    assert (1 << shift) == c and seq % sb == 0
    row = lax.broadcasted_iota(jnp.int32, (sb, sb), 0)
    col = lax.broadcasted_iota(jnp.int32, (sb, sb), 1)
    same_chunk = (row >> shift) == (col >> shift)
    causal = same_chunk & (row >= col)
    tril = causal.astype(F32)
    ones_bd = same_chunk.astype(F32)
    wlr = wlr_ref[...]
    bfg = bf_ref[...]
    gn = gn_ref[...]
    q_scale = GLA_DK ** -0.5

    for s in range(nsb):
        rows = slice(s * sb, (s + 1) * sb)
        v = v_ref[rows, :]
        xa = _dot(alr_ref[rows, :], wlr, HIGHEST) + bfg
        log_a = (jnp.minimum(xa, 0.0) - jnp.log1p(jnp.exp(-jnp.abs(xa)))) * (1.0 / GLA_TAU)
        b = _dot(tril, log_a, HIGHEST)
        b_last = _dot(ones_bd, log_a, HIGHEST)
        q_dec = ((q_ref[rows, :] * q_scale) * jnp.exp(b)).astype(BF16)
        k = k_ref[rows, :]
        k_dec = (k * jnp.exp(-b)).astype(BF16)
        k_end = (k * jnp.exp(b_last - b)).astype(BF16)
        attn = jnp.where(causal, _dot_nt(q_dec, k_dec), 0.0).astype(BF16)
        oi_ref[rows, :] = _dot(attn, v)
        qd_ref[rows, :] = q_dec
        chunk_decay = jnp.exp(b_last)
        for cc in range(cps):
            n = s * cps + cc
            u_ref[n] = _dot_tn(v[cc * c:(cc + 1) * c, :], k_end[cc * c:(cc + 1) * c, :])
            dec_ref[n] = chunk_decay[cc * c:cc * c + 8, :]

    def scan(n, st):
        sb_ref[n] = st.astype(BF16)
        return st * dec_ref[n, 0:1, :] + u_ref[n]

    lax.fori_loop(0, nchunks, scan, jnp.zeros((GLA_DV, GLA_DK), F32), unroll=4)

    for s in range(nsb):
        rows = slice(s * sb, (s + 1) * sb)
        inter = [_dot_nt(qd_ref[(s * cps + cc) * c:(s * cps + cc + 1) * c, :], sb_ref[s * cps + cc])
                 for cc in range(cps)]
        o = oi_ref[rows, :] + jnp.concatenate(inter, axis=0)
        ms = jnp.mean(o * o, axis=-1, keepdims=True)
        r = r_ref[rows, :]
        out = (o * lax.rsqrt(ms + EPS) * gn) * (r * _sigmoid(r))
        o_ref[rows, :] = out.astype(BF16)


def _gla(pf3, pb3, wlr, bforget, gnorm, layer, *, fcols, bcols):
    bsz, seq, _ = pf3.shape
    qa0, ka0, ra0, alr0 = (fcols[k][0] for k in ("qa", "ka", "ra", "alr"))
    va0 = bcols["va"][0]
    kern = functools.partial(_gla_kernel, seq=seq)
    return pl.pallas_call(
        kern,
        grid=(bsz, GLA_HEADS),
        in_specs=[
            pl.BlockSpec((None, seq, GLA_DK), lambda b, h: (b, 0, qa0 // GLA_DK + h)),
            pl.BlockSpec((None, seq, GLA_DK), lambda b, h: (b, 0, ka0 // GLA_DK + h)),
            pl.BlockSpec((None, seq, GLA_DV), lambda b, h: (b, 0, va0 // GLA_DV + h)),
            pl.BlockSpec((None, seq, GLA_DV), lambda b, h: (b, 0, ra0 // GLA_DV + h)),
            pl.BlockSpec((None, seq, ALR_PAD), lambda b, h: (b, 0, alr0 // ALR_PAD)),
            pl.BlockSpec((None, ALR_PAD, GLA_DK), lambda b, h: (layer, 0, h)),
            pl.BlockSpec((None, 1, GLA_DK), lambda b, h: (layer, 0, h)),
            pl.BlockSpec((None, 1, GLA_DV), lambda b, h: (layer, 0, h)),
        ],
        out_specs=pl.BlockSpec((None, seq, GLA_DV), lambda b, h: (b, 0, h)),
        out_shape=jax.ShapeDtypeStruct((bsz, seq, GLA_V_W), BF16),
        scratch_shapes=[
            pltpu.VMEM((seq, GLA_DK), BF16),
            pltpu.VMEM((seq // GLA_CHUNK, GLA_DV, GLA_DK), F32),
            pltpu.VMEM((seq // GLA_CHUNK, 8, GLA_DK), F32),
            pltpu.VMEM((seq, GLA_DV), F32),
            pltpu.VMEM((seq // GLA_CHUNK, GLA_DV, GLA_DK), BF16),
        ],
        compiler_params=_params("parallel", "parallel"),
        name="gla",
    )(pf3, pf3, pb3, pf3, pf3, wlr, bforget, gnorm)


def _t5_bucket(rel):
    n = jnp.maximum(rel, 0)
    max_exact = REL_BUCKETS // 2
    nf = jnp.maximum(n, 1).astype(F32)
    large = max_exact + (jnp.log(nf / max_exact) / math.log(REL_MAX_DIST / max_exact)
                         * (REL_BUCKETS - max_exact)).astype(jnp.int32)
    large = jnp.minimum(large, REL_BUCKETS - 1)
    return jnp.where(n < max_exact, n, large)


def _far_bucket(seq):
    d = np.arange(MOBA_BLOCK + 1, max(seq, MOBA_BLOCK + 2), dtype=np.float64)
    max_exact = REL_BUCKETS // 2
    large = max_exact + np.floor(np.log(d / max_exact) / math.log(REL_MAX_DIST / max_exact)
                                 * (REL_BUCKETS - max_exact) + 1e-6)
    assert MOBA_BLOCK + 1 >= max_exact and np.all(large >= REL_BUCKETS - 1), "far blocks must share one bucket"
    return REL_BUCKETS - 1


def _bias_kernel(bucket_ref, rb_ref, o_ref, *, far_bucket):
    h = pl.program_id(0)
    blk = MOBA_BLOCK
    far = rb_ref[far_bucket, h]
    row = lax.broadcasted_iota(jnp.int32, (blk, blk), 0)
    col = lax.broadcasted_iota(jnp.int32, (blk, blk), 1)
    o_ref[0] = jnp.zeros((blk, blk), F32)
    for t in range(2):
        bk = bucket_ref[t]
        acc = jnp.zeros((blk, blk), F32)
        for b in range(REL_BUCKETS):
            acc = jnp.where(bk == b, rb_ref[b, h], acc)
        acc = acc - far
        if t == 1:
            acc = jnp.where(col >= row, acc, NEG_INF)
        o_ref[t + 1] = acc


def _bias_tiles(rel_bias, seq):
    blk = MOBA_BLOCK
    kpos = jnp.arange(blk, dtype=jnp.int32)[:, None]
    qpos = jnp.arange(blk, dtype=jnp.int32)[None, :]
    bucket = jnp.stack([_t5_bucket(qpos + blk - kpos), _t5_bucket(qpos - kpos)])
    kern = functools.partial(_bias_kernel, far_bucket=_far_bucket(seq))
    return pl.pallas_call(
        kern,
        grid=(MOBA_HEADS,),
        in_specs=[
            pl.BlockSpec((2, blk, blk), lambda h: (0, 0, 0)),
            pl.BlockSpec(memory_space=pltpu.SMEM),
        ],
        out_specs=pl.BlockSpec((None, 3, blk, blk), lambda h: (h, 0, 0, 0)),
        out_shape=jax.ShapeDtypeStruct((MOBA_HEADS, 3, blk, blk), F32),
        compiler_params=_params("arbitrary"),
        name="rel_bias_tiles",
    )(bucket, rel_bias)


PEN_ROWS = 16


def _moba_kernel(qt_ref, k_ref, vt_ref, bias_ref, o_ref, kaug_ref, kmean_ref, qaug_ref, acc_ref, *, nb):
    i = pl.program_id(1)
    blk, dh, heads = MOBA_BLOCK, MOBA_DH, MOBA_HEADS
    seq = nb * blk
    pair_w = 2 * dh
    assert pair_w == LANES and nb <= PEN_ROWS <= dh
    blk_shift = blk.bit_length() - 1
    assert (1 << blk_shift) == blk
    lane = lax.broadcasted_iota(jnp.int32, (1, pair_w), 1)

    @pl.when(i == 0)
    def _build_keys():
        rowblk = lax.broadcasted_iota(jnp.int32, (seq, pair_w), 0) >> blk_shift
        lane_s = lax.broadcasted_iota(jnp.int32, (seq, pair_w), 1)
        for p in range(heads // 2):
            kp = k_ref[:, p * pair_w:(p + 1) * pair_w]
            km = jnp.sum(kp.astype(F32).reshape(nb, blk, pair_w), axis=1) * (1.0 / blk)
            kmean_ref[p] = km
            for e in range(2):
                own = (lane_s >= dh) if e else (lane_s < dh)
                onehot = (lane_s - (0 if e else dh)) == rowblk
                kaug_ref[2 * p + e] = jnp.where(own, kp, onehot.astype(BF16))

    row8 = lax.broadcasted_iota(jnp.int32, (nb, blk), 0)
    scale = jnp.asarray(dh ** -0.5, BF16)

    for p in range(heads // 2):
        qtp = qt_ref[p * pair_w:(p + 1) * pair_w, :]
        km = kmean_ref[p]
        for e in range(2):
            own = (lane >= dh) if e else (lane < dh)
            kme = jnp.where(own, km, 0.0)
            km_hi = kme.astype(BF16)
            km_lo = (kme - km_hi.astype(F32)).astype(BF16)
            g = _dot(km_hi, qtp) + _dot(km_lo, qtp)
            g = jnp.where(row8 < i, g, -jnp.inf)
            rank = jnp.zeros((nb, blk), jnp.int32)
            for m in range(nb):
                gm = g[m:m + 1, :]
                beats = (gm > g) | ((gm == g) & (m < row8))
                rank = rank + beats.astype(jnp.int32)
            keep = ((rank < MOBA_TOPK) & (row8 < i)) | (row8 == i)
            pen_t = jnp.where(keep, 0.0, NEG_INF)
            pen = jnp.concatenate([pen_t, jnp.zeros((PEN_ROWS - nb, blk), F32)], axis=0).astype(BF16)
            qs = qtp[e * dh:(e + 1) * dh, :] * scale
            if e == 0:
                parts = [qs, pen, jnp.zeros((pair_w - dh - PEN_ROWS, blk), BF16)]
            else:
                parts = [pen, jnp.zeros((dh - PEN_ROWS, blk), BF16), qs]
            qaug_ref[2 * p + e] = jnp.concatenate(parts, axis=0)

    acc_ref[...] = jnp.zeros_like(acc_ref)

    def jbody(j, carry):
        ms, ls = carry
        k0 = pl.multiple_of(j * blk, blk)
        t = jnp.clip(j - i + 2, 0, 2)
        new_m, new_l = [], []
        for h in range(heads):
            s = _dot(kaug_ref[h, pl.ds(k0, blk), :], qaug_ref[h])
            s = s + bias_ref[h, t]
            m_new = jnp.maximum(ms[h], jnp.max(s, axis=0, keepdims=True))
            alpha = jnp.exp(ms[h] - m_new)
            pr = jnp.exp(s - m_new)
            new_l.append(alpha * ls[h] + jnp.sum(pr, axis=0, keepdims=True))
            new_m.append(m_new)
            vt = vt_ref[j, h * dh:(h + 1) * dh, :]
            acc_ref[h] = alpha * acc_ref[h] + _dot(vt, pr.astype(BF16))
        return tuple(new_m), tuple(new_l)

    init = (tuple(jnp.full((1, blk), -jnp.inf, F32) for _ in range(heads)),
            tuple(jnp.zeros((1, blk), F32) for _ in range(heads)))
    _, l_f = lax.fori_loop(0, i + 1, jbody, init)
    for h in range(heads):
        o_ref[h * dh:(h + 1) * dh, :] = (acc_ref[h] / l_f[h]).astype(BF16)


def _moba(qt, pb3, vt, bias, *, bcols):
    bsz, seq, _ = pb3.shape
    blk = MOBA_BLOCK
    nb = seq // blk
    assert seq % blk == 0
    kb0 = bcols["kb"][0]
    kern = functools.partial(_moba_kernel, nb=nb)
    return pl.pallas_call(
        kern,
        grid=(bsz, nb),
        in_specs=[
            pl.BlockSpec((None, None, MOBA_W, blk), lambda b, i: (b, i, 0, 0)),
            pl.BlockSpec((None, seq, MOBA_W), lambda b, i: (b, 0, kb0 // MOBA_W)),
            pl.BlockSpec((None, nb, MOBA_W, blk), lambda b, i: (b, 0, 0, 0)),
            _resident((MOBA_HEADS, 3, blk, blk), lambda b, i: (0, 0, 0, 0)),
        ],
        out_specs=pl.BlockSpec((None, None, MOBA_W, blk), lambda b, i: (b, i, 0, 0)),
        out_shape=jax.ShapeDtypeStruct((bsz, nb, MOBA_W, blk), BF16),
        scratch_shapes=[
            pltpu.VMEM((MOBA_HEADS, seq, 2 * MOBA_DH), BF16),
            pltpu.VMEM((MOBA_HEADS // 2, nb, 2 * MOBA_DH), F32),
            pltpu.VMEM((MOBA_HEADS, 2 * MOBA_DH, blk), BF16),
            pltpu.VMEM((MOBA_HEADS, MOBA_DH, blk), F32),
        ],
        compiler_params=_params("arbitrary", "arbitrary"),
        name="moba",
    )(qt, pb3, vt, bias)


def _merge_kernel(oa_ref, obt_ref, g_ref, x_ref, wg_ref, wm_ref, wo_ref, o_ref, *, d):
    ya = _dot(oa_ref[...], wg_ref[...])
    wm = wm_ref[...]
    yb = jnp.concatenate([_dot_tn(obt_ref[s], wm) for s in range(obt_ref.shape[0])], axis=0)
    mixed = _sigmoid(g_ref[:, 0:d]) * ya + _sigmoid(g_ref[:, d:2 * d]) * yb
    o_ref[...] = x_ref[...] + _dot(mixed.astype(BF16), wo_ref[...])


def _merge(oa2, obt, pf2, x2, wg, wm, wo, layer, *, seq, tm):
    n, d = x2.shape
    blk = MOBA_BLOCK
    tpb = seq // tm
    kern = functools.partial(_merge_kernel, d=d)
    return pl.pallas_call(
        kern,
        grid=(n // tm,),
        in_specs=[
            pl.BlockSpec((tm, GLA_V_W), lambda i: (i, 0)),
            pl.BlockSpec((None, tm // blk, MOBA_W, blk), lambda i: (i // tpb, i % tpb, 0, 0)),
            pl.BlockSpec((tm, D_MERGE * d), lambda i: (i, 0)),
            pl.BlockSpec((tm, d), lambda i: (i, 0)),
            _resident((None, GLA_V_W, d), lambda i: (layer, 0, 0)),
            _resident((None, MOBA_W, d), lambda i: (layer, 0, 0)),
            _resident((None, d, d), lambda i: (layer, 0, 0)),
        ],
        out_specs=pl.BlockSpec((tm, d), lambda i: (i, 0)),
        out_shape=jax.ShapeDtypeStruct((n, d), F32),
        compiler_params=_params("parallel"),
        name="merge_out",
    )(oa2, obt, pf2, x2, wg, wm, wo)


HALO = 8


def _ffn_kernel(x_ref, g_ref, wu_ref, cw_ref, cb_ref, wd_ref, gf_ref, o_ref, ubuf_ref, carry_ref,
                *, d_ff, cw, final):
    t = pl.program_id(1)
    tm = x_ref.shape[0]

    @pl.when(t == 0)
    def _zero_history():
        carry_ref[...] = jnp.zeros_like(carry_ref)

    x = x_ref[...]
    ms = jnp.mean(x * x, axis=-1, keepdims=True)
    h = (x * lax.rsqrt(ms + EPS) * g_ref[...]).astype(BF16)
    acc = jnp.zeros((tm, x.shape[1]), F32)
    for c in range(d_ff // cw):
        halves = []
        for part in range(2):
            c0 = part * d_ff + c * cw
            u = _dot(h, wu_ref[:, c0:c0 + cw])
            ubuf_ref[0:HALO, :] = carry_ref[:, c0:c0 + cw]
            ubuf_ref[HALO:HALO + tm, :] = u
            carry_ref[:, c0:c0 + cw] = u[tm - HALO:tm, :]
            w = cw_ref[:, c0:c0 + cw]
            conv = cb_ref[:, c0:c0 + cw] + w[CONV_W - 1:CONV_W, :] * u
            for s in range(1, CONV_W):
                conv = conv + w[CONV_W - 1 - s:CONV_W - s, :] * ubuf_ref[HALO - s:HALO - s + tm, :]
            halves.append(conv)
        a, bval = halves
        act = ((a * _sigmoid(a)) * bval).astype(BF16)
        acc = acc + _dot(act, wd_ref[c * cw:(c + 1) * cw, :])
    y = x + acc
    if final:
        ms2 = jnp.mean(y * y, axis=-1, keepdims=True)
        y = y * lax.rsqrt(ms2 + EPS) * gf_ref[...]
    o_ref[...] = y


def _ffn(x3, g, wu, cw_, cb, wd, gfinal, layer, *, tm, final):
    bsz, seq, d = x3.shape
    d_ff = wd.shape[1]
    cw = 256
    assert d_ff % cw == 0 and seq % tm == 0
    kern = functools.partial(_ffn_kernel, d_ff=d_ff, cw=cw, final=final)
    return pl.pallas_call(
        kern,
        grid=(bsz, seq // tm),
        in_specs=[
            pl.BlockSpec((None, tm, d), lambda b, t: (b, t, 0)),
            pl.BlockSpec((None, 1, d), lambda b, t: (layer, 0, 0)),
            _resident((None, d, 2 * d_ff), lambda b, t: (layer, 0, 0)),
            pl.BlockSpec((None, CONV_W, 2 * d_ff), lambda b, t: (layer, 0, 0)),
            pl.BlockSpec((None, 1, 2 * d_ff), lambda b, t: (layer, 0, 0)),
            _resident((None, d_ff, d), lambda b, t: (layer, 0, 0)),
            pl.BlockSpec((1, d), lambda b, t: (0, 0)),
        ],
        out_specs=pl.BlockSpec((None, tm, d), lambda b, t: (b, t, 0)),
        out_shape=jax.ShapeDtypeStruct((bsz, seq, d), F32),
        scratch_shapes=[
            pltpu.VMEM((HALO + tm, cw), F32),
            pltpu.VMEM((HALO, 2 * d_ff), F32),
        ],
        compiler_params=_params("arbitrary", "arbitrary"),
        name="ffn",
    )(x3, g, wu, cw_, cb, wd, gfinal)


def kernel(x, rel_bias, norm_mix, w_in, w_lr_up, b_forget, gla_out_norm, w_branch_gla, w_branch_moba,
           w_out, norm_ffn, w_up, conv_w, conv_b, w_down, norm_final):
    bsz, seq, d = x.shape
    depth = w_in.shape[0]
    n = bsz * seq
    fcols, wf, bcols, wb = _proj_layout(d)

    w_in_r, w_in_t = _regroup_w_in(w_in, d)
    wlr = jnp.pad(w_lr_up, ((0, 0), (0, ALR_PAD - GLA_RANK), (0, 0)))
    wg = w_branch_gla.astype(BF16)
    wm = w_branch_moba.astype(BF16)
    wo = w_out.astype(BF16)
    wu = w_up.astype(BF16)
    wd = w_down.astype(BF16)
    norm_mix3 = norm_mix[:, None, :]
    norm_ffn3 = norm_ffn[:, None, :]
    b_forget3 = b_forget[:, None, :]
    gnorm3 = gla_out_norm[:, None, :]
    conv_b3 = conv_b[:, None, :]
    gfinal = norm_final[None, :]

    bias = _bias_tiles(rel_bias, seq)

    for l in range(depth):
        pf, pb, qt, vt = _in_proj(x.reshape(n, d), norm_mix3, w_in_r, w_in_t, l, seq=seq, wf=wf, wb=wb, tm=512)
        pf3 = pf.reshape(bsz, seq, wf)
        pb3 = pb.reshape(bsz, seq, wb)
        oa = _gla(pf3, pb3, wlr, b_forget3, gnorm3, l, fcols=fcols, bcols=bcols)
        obt = _moba(qt, pb3, vt, bias, bcols=bcols)
        x1 = _merge(oa.reshape(n, GLA_V_W), obt, pf, x.reshape(n, d), wg, wm, wo, l, seq=seq, tm=512)
        x = _ffn(x1.reshape(bsz, seq, d), norm_ffn3, wu, conv_w, conv_b3, wd, gfinal, l,
                 tm=512, final=(l == depth - 1))
    return x
```

```python
import functools
import math

import jax
import jax.numpy as jnp
import numpy as np
from jax import lax
from jax.experimental import pallas as pl
from jax.experimental.pallas import tpu as pltpu

GLA_HEADS = 4
GLA_DK = 128
GLA_DV = 256
GLA_RANK = 16
GLA_TAU = 16.0
GLA_CHUNK = 64
MOBA_HEADS = 8
MOBA_DH = 64
MOBA_BLOCK = 256
MOBA_TOPK = 3
REL_BUCKETS = 32
REL_MAX_DIST = 128
CONV_W = 3
EPS = 1e-6
NEG_INF = -1e30

GLA_QK_W = GLA_HEADS * GLA_DK
GLA_V_W = GLA_HEADS * GLA_DV
MOBA_W = MOBA_HEADS * MOBA_DH

LANES = 128
VMEM_LIMIT_BYTES = 56 * 1024 * 1024

F32 = jnp.float32
BF16 = jnp.bfloat16
HIGHEST = lax.Precision.HIGHEST

_NT = (((1,), (1,)), ((), ()))
_TN = (((0,), (0,)), ((), ()))


def _dot(a, b, precision=None):
    return jnp.dot(a, b, preferred_element_type=F32, precision=precision)


def _dot_nt(a, b, precision=None):
    return lax.dot_general(a, b, _NT, preferred_element_type=F32, precision=precision)


def _dot_tn(a, b):
    return lax.dot_general(a, b, _TN, preferred_element_type=F32)


def _split_bf16(x, n):
    parts = []
    for _ in range(n - 1):
        p = x.astype(BF16)
        parts.append(p)
        x = x - p.astype(F32)
    parts.append(x.astype(BF16))
    return parts


def _sigmoid(x):
    return 1.0 / (1.0 + jnp.exp(-x))


def _params(*semantics):
    return pltpu.CompilerParams(dimension_semantics=semantics, vmem_limit_bytes=VMEM_LIMIT_BYTES)


def _resident(shape, index_map):
    return pl.BlockSpec(shape, index_map, pipeline_mode=pl.Buffered(1))


D_MERGE = 2
ALR_PAD = LANES


def _proj_layout(d_model):
    f32_cols = dict(gates=(0, D_MERGE * d_model))
    off = D_MERGE * d_model
    for name, w in (("ra", GLA_V_W), ("qa", GLA_QK_W), ("ka", GLA_QK_W), ("alr", ALR_PAD)):
        f32_cols[name] = (off, w)
        off += w
    wf = off
    bf_cols = {}
    off = 0
    for name, w in (("va", GLA_V_W), ("kb", MOBA_W)):
        bf_cols[name] = (off, w)
        off += w
    return f32_cols, wf, bf_cols, off


def _regroup_w_in(w_in, d_model):
    splits = (GLA_QK_W, GLA_QK_W, GLA_V_W, GLA_V_W, GLA_RANK, MOBA_W, MOBA_W, MOBA_W, D_MERGE * d_model)
    pts = np.cumsum(splits)[:-1].tolist()
    qa, ka, va, ra, alr, qb, kb, vb, gates = jnp.split(w_in, pts, axis=-1)
    alr = jnp.pad(alr, ((0, 0), (0, 0), (0, ALR_PAD - GLA_RANK)))
    w_rows = jnp.concatenate([gates, ra, qa, ka, alr, va, kb], axis=-1).astype(BF16)
    w_t = jnp.swapaxes(jnp.concatenate([qb, vb], axis=-1), 1, 2).astype(BF16)
    return w_rows, w_t


def _in_proj_kernel(x_ref, g_ref, w_ref, wt_ref, of_ref, ob_ref, qt_ref, vt_ref, *, wf, wb, chunk):
    x = x_ref[...]
    ms = jnp.mean(x * x, axis=-1, keepdims=True)
    h = (x * lax.rsqrt(ms + EPS) * g_ref[...]).astype(BF16)
    for c0 in range(0, wf, chunk):
        c1 = min(c0 + chunk, wf)
        of_ref[:, c0:c1] = _dot(h, w_ref[:, c0:c1])
    for c0 in range(0, wb, chunk):
        c1 = min(c0 + chunk, wb)
        ob_ref[:, c0:c1] = _dot(h, w_ref[:, wf + c0:wf + c1]).astype(BF16)
    blk = MOBA_BLOCK
    for s in range(x.shape[0] // blk):
        hs = h[s * blk:(s + 1) * blk, :]
        qt_ref[s] = _dot_nt(wt_ref[0:MOBA_W, :], hs).astype(BF16)
        vt_ref[s] = _dot_nt(wt_ref[MOBA_W:2 * MOBA_W, :], hs).astype(BF16)


def _in_proj(x2, g, w, wt, layer, *, seq, wf, wb, tm):
    n, d = x2.shape
    blk = MOBA_BLOCK
    tpb = seq // tm
    spt = tm // blk
    kern = functools.partial(_in_proj_kernel, wf=wf, wb=wb, chunk=512)
    t_shape = jax.ShapeDtypeStruct((n // seq, seq // blk, MOBA_W, blk), BF16)
    t_spec = pl.BlockSpec((None, spt, MOBA_W, blk), lambda i: (i // tpb, i % tpb, 0, 0))
    return pl.pallas_call(
        kern,
        grid=(n // tm,),
        in_specs=[
            pl.BlockSpec((tm, d), lambda i: (i, 0)),
            pl.BlockSpec((None, 1, d), lambda i: (layer, 0, 0)),
            _resident((None, d, wf + wb), lambda i: (layer, 0, 0)),
            _resident((None, 2 * MOBA_W, d), lambda i: (layer, 0, 0)),
        ],
        out_specs=[
            pl.BlockSpec((tm, wf), lambda i: (i, 0)),
            pl.BlockSpec((tm, wb), lambda i: (i, 0)),
            t_spec,
            t_spec,
        ],
        out_shape=[jax.ShapeDtypeStruct((n, wf), F32), jax.ShapeDtypeStruct((n, wb), BF16), t_shape, t_shape],
        compiler_params=_params("parallel"),
        name="in_proj",
    )(x2, g, w, wt)


GLA_SUPER = 256


def _gla_kernel(q_ref, k_ref, v_ref, r_ref, alr_ref, wlr_ref, bf_ref, gn_ref, o_ref,
                qd_ref, u_ref, dec_ref, oi_ref, sb_ref, b_ref, kd_ref, ke_ref, at_ref, *, seq):
    c, sb = GLA_CHUNK, GLA_SUPER
    cps = sb // c
    nsb = seq // sb
    nchunks = seq // c
    shift = c.bit_length() - 1
    assert (1 << shift) == c and seq % sb == 0
    row = lax.broadcasted_iota(jnp.int32, (sb, sb), 0)
    col = lax.broadcasted_iota(jnp.int32, (sb, sb), 1)
    same_chunk = (row >> shift) == (col >> shift)
    causal = same_chunk & (row >= col)
    tril = causal.astype(BF16)
    wlr_parts = _split_bf16(wlr_ref[...], 2)
    bfg = bf_ref[...]
    gn = gn_ref[...]
    q_scale = GLA_DK ** -0.5

    blocks = [slice(s * sb, (s + 1) * sb) for s in range(nsb)]

    for rows in blocks:
        a_hi, a_lo = _split_bf16(alr_ref[rows, :], 2)
        xa = _dot(a_hi, wlr_parts[0]) + _dot(a_hi, wlr_parts[1]) + _dot(a_lo, wlr_parts[0]) + bfg
        log_a = (jnp.minimum(xa, 0.0) - jnp.log1p(jnp.exp(-jnp.abs(xa)))) * (1.0 / GLA_TAU)
        pieces = _dot(tril, jnp.concatenate(_split_bf16(log_a, 3), axis=1))
        b_ref[rows, :] = (pieces[:, 0:GLA_DK] + pieces[:, GLA_DK:2 * GLA_DK]) + pieces[:, 2 * GLA_DK:3 * GLA_DK]

    for s, rows in enumerate(blocks):
        b = b_ref[rows, :]
        b_last = jnp.concatenate(
            [jnp.broadcast_to(b[(cc + 1) * c - 1:(cc + 1) * c, :], (c, GLA_DK)) for cc in range(cps)], axis=0)
        k = k_ref[rows, :]
        qd_ref[rows, :] = ((q_ref[rows, :] * q_scale) * jnp.exp(b)).astype(BF16)
        kd_ref[rows, :] = (k * jnp.exp(-b)).astype(BF16)
        ke_ref[rows, :] = (k * jnp.exp(b_last - b)).astype(BF16)
        chunk_decay = jnp.exp(b_last)
        for cc in range(cps):
            dec_ref[s * cps + cc] = chunk_decay[cc * c:cc * c + 8, :]

    for rows in blocks:
        at_ref[rows, :] = jnp.where(causal, _dot_nt(qd_ref[rows, :], kd_ref[rows, :]), 0.0).astype(BF16)

    for n in range(nchunks):
        u_ref[n] = _dot_tn(v_ref[n * c:(n + 1) * c, :], ke_ref[n * c:(n + 1) * c, :])

    for rows in blocks:
        oi_ref[rows, :] = _dot(at_ref[rows, :], v_ref[rows, :])

    def scan(n, st):
        sb_ref[n] = st.astype(BF16)
        return st * dec_ref[n, 0:1, :] + u_ref[n]

    lax.fori_loop(0, nchunks, scan, jnp.zeros((GLA_DV, GLA_DK), F32), unroll=4)

    for n in range(nchunks):
        crow = slice(n * c, (n + 1) * c)
        oi_ref[crow, :] = oi_ref[crow, :] + _dot_nt(qd_ref[crow, :], sb_ref[n])

    for rows in blocks:
        o = oi_ref[rows, :]
        ms = jnp.mean(o * o, axis=-1, keepdims=True)
        r = r_ref[rows, :]
        out = (o * lax.rsqrt(ms + EPS) * gn) * (r * _sigmoid(r))
        o_ref[rows, :] = out.astype(BF16)


def _gla(pf3, pb3, wlr, bforget, gnorm, layer, *, fcols, bcols):
    bsz, seq, _ = pf3.shape
    qa0, ka0, ra0, alr0 = (fcols[k][0] for k in ("qa", "ka", "ra", "alr"))
    va0 = bcols["va"][0]
    kern = functools.partial(_gla_kernel, seq=seq)
    return pl.pallas_call(
        kern,
        grid=(bsz, GLA_HEADS),
        in_specs=[
            pl.BlockSpec((None, seq, GLA_DK), lambda b, h: (b, 0, qa0 // GLA_DK + h)),
            pl.BlockSpec((None, seq, GLA_DK), lambda b, h: (b, 0, ka0 // GLA_DK + h)),
            pl.BlockSpec((None, seq, GLA_DV), lambda b, h: (b, 0, va0 // GLA_DV + h)),
            pl.BlockSpec((None, seq, GLA_DV), lambda b, h: (b, 0, ra0 // GLA_DV + h)),
            pl.BlockSpec((None, seq, ALR_PAD), lambda b, h: (b, 0, alr0 // ALR_PAD)),
            pl.BlockSpec((None, ALR_PAD, GLA_DK), lambda b, h: (layer, 0, h)),
            pl.BlockSpec((None, 1, GLA_DK), lambda b, h: (layer, 0, h)),
            pl.BlockSpec((None, 1, GLA_DV), lambda b, h: (layer, 0, h)),
        ],
        out_specs=pl.BlockSpec((None, seq, GLA_DV), lambda b, h: (b, 0, h)),
        out_shape=jax.ShapeDtypeStruct((bsz, seq, GLA_V_W), BF16),
        scratch_shapes=[
            pltpu.VMEM((seq, GLA_DK), BF16),
            pltpu.VMEM((seq // GLA_CHUNK, GLA_DV, GLA_DK), F32),
            pltpu.VMEM((seq // GLA_CHUNK, 8, GLA_DK), F32),
            pltpu.VMEM((seq, GLA_DV), F32),
            pltpu.VMEM((seq // GLA_CHUNK, GLA_DV, GLA_DK), BF16),
            pltpu.VMEM((seq, GLA_DK), F32),
            pltpu.VMEM((seq, GLA_DK), BF16),
            pltpu.VMEM((seq, GLA_DK), BF16),
            pltpu.VMEM((seq, GLA_SUPER), BF16),
        ],
        compiler_params=_params("parallel", "parallel"),
        name="gla",
    )(pf3, pf3, pb3, pf3, pf3, wlr, bforget, gnorm)


def _t5_bucket(rel):
    n = jnp.maximum(rel, 0)
    max_exact = REL_BUCKETS // 2
    nf = jnp.maximum(n, 1).astype(F32)
    large = max_exact + (jnp.log(nf / max_exact) / math.log(REL_MAX_DIST / max_exact)
                         * (REL_BUCKETS - max_exact)).astype(jnp.int32)
    large = jnp.minimum(large, REL_BUCKETS - 1)
    return jnp.where(n < max_exact, n, large)


def _far_bucket(seq):
    d = np.arange(MOBA_BLOCK + 1, max(seq, MOBA_BLOCK + 2), dtype=np.float64)
    max_exact = REL_BUCKETS // 2
    large = max_exact + np.floor(np.log(d / max_exact) / math.log(REL_MAX_DIST / max_exact)
                                 * (REL_BUCKETS - max_exact) + 1e-6)
    assert MOBA_BLOCK + 1 >= max_exact and np.all(large >= REL_BUCKETS - 1), "far blocks must share one bucket"
    return REL_BUCKETS - 1


def _bias_kernel(bucket_ref, rb_ref, o_ref, *, far_bucket):
    h = pl.program_id(0)
    blk = MOBA_BLOCK
    far = rb_ref[far_bucket, h]
    row = lax.broadcasted_iota(jnp.int32, (blk, blk), 0)
    col = lax.broadcasted_iota(jnp.int32, (blk, blk), 1)
    o_ref[0] = jnp.zeros((blk, blk), F32)
    for t in range(2):
        bk = bucket_ref[t]
        acc = jnp.zeros((blk, blk), F32)
        for b in range(REL_BUCKETS):
            acc = jnp.where(bk == b, rb_ref[b, h], acc)
        acc = acc - far
        if t == 1:
            acc = jnp.where(col >= row, acc, NEG_INF)
        o_ref[t + 1] = acc


def _bias_tiles(rel_bias, seq):
    blk = MOBA_BLOCK
    kpos = jnp.arange(blk, dtype=jnp.int32)[:, None]
    qpos = jnp.arange(blk, dtype=jnp.int32)[None, :]
    bucket = jnp.stack([_t5_bucket(qpos + blk - kpos), _t5_bucket(qpos - kpos)])
    kern = functools.partial(_bias_kernel, far_bucket=_far_bucket(seq))
    return pl.pallas_call(
        kern,
        grid=(MOBA_HEADS,),
        in_specs=[
            pl.BlockSpec((2, blk, blk), lambda h: (0, 0, 0)),
            pl.BlockSpec(memory_space=pltpu.SMEM),
        ],
        out_specs=pl.BlockSpec((None, 3, blk, blk), lambda h: (h, 0, 0, 0)),
        out_shape=jax.ShapeDtypeStruct((MOBA_HEADS, 3, blk, blk), F32),
        compiler_params=_params("arbitrary"),
        name="rel_bias_tiles",
    )(bucket, rel_bias)


PEN_ROWS = 16


def _moba_kernel(qt_ref, k_ref, vt_ref, bias_ref, o_ref, kaug_ref, kmean_ref, qaug_ref, acc_ref, s_ref, *, nb):
    i = pl.program_id(1)
    blk, dh, heads = MOBA_BLOCK, MOBA_DH, MOBA_HEADS
    seq = nb * blk
    pair_w = 2 * dh
    assert pair_w == LANES and nb <= PEN_ROWS <= dh
    blk_shift = blk.bit_length() - 1
    assert (1 << blk_shift) == blk
    lane = lax.broadcasted_iota(jnp.int32, (1, pair_w), 1)

    @pl.when(i == 0)
    def _build_keys():
        rowblk = lax.broadcasted_iota(jnp.int32, (seq, pair_w), 0) >> blk_shift
        lane_s = lax.broadcasted_iota(jnp.int32, (seq, pair_w), 1)
        for p in range(heads // 2):
            kp = k_ref[:, p * pair_w:(p + 1) * pair_w]
            km = jnp.sum(kp.astype(F32).reshape(nb, blk, pair_w), axis=1) * (1.0 / blk)
            kmean_ref[p] = km
            for e in range(2):
                own = (lane_s >= dh) if e else (lane_s < dh)
                onehot = (lane_s - (0 if e else dh)) == rowblk
                kaug_ref[2 * p + e] = jnp.where(own, kp, onehot.astype(BF16))

    row8 = lax.broadcasted_iota(jnp.int32, (nb, blk), 0)
    scale = jnp.asarray(dh ** -0.5, BF16)

    for p in range(heads // 2):
        qtp = qt_ref[p * pair_w:(p + 1) * pair_w, :]
        km = kmean_ref[p]
        for e in range(2):
            own = (lane >= dh) if e else (lane < dh)
            kme = jnp.where(own, km, 0.0)
            km_hi = kme.astype(BF16)
            km_lo = (kme - km_hi.astype(F32)).astype(BF16)
            g = _dot(km_hi, qtp) + _dot(km_lo, qtp)
            g = jnp.where(row8 < i, g, -jnp.inf)
            rank = jnp.zeros((nb, blk), jnp.int32)
            for m in range(nb):
                gm = g[m:m + 1, :]
                beats = (gm > g) | ((gm == g) & (m < row8))
                rank = rank + beats.astype(jnp.int32)
            keep = ((rank < MOBA_TOPK) & (row8 < i)) | (row8 == i)
            pen_t = jnp.where(keep, 0.0, NEG_INF)
            pen = jnp.concatenate([pen_t, jnp.zeros((PEN_ROWS - nb, blk), F32)], axis=0).astype(BF16)
            qs = qtp[e * dh:(e + 1) * dh, :] * scale
            if e == 0:
                parts = [qs, pen, jnp.zeros((pair_w - dh - PEN_ROWS, blk), BF16)]
            else:
                parts = [pen, jnp.zeros((dh - PEN_ROWS, blk), BF16), qs]
            qaug_ref[2 * p + e] = jnp.concatenate(parts, axis=0)

    acc_ref[...] = jnp.zeros_like(acc_ref)

    def jbody(j, carry):
        ms, ls = carry
        k0 = pl.multiple_of(j * blk, blk)
        t = jnp.clip(j - i + 2, 0, 2)
        new_m, new_l = [], []
        for h in range(heads):
            s = _dot(kaug_ref[h, pl.ds(k0, blk), :], qaug_ref[h])
            s = s + bias_ref[h, t]
            s_ref[h] = s
            new_m.append(jnp.maximum(ms[h], jnp.max(s, axis=0, keepdims=True)))
        for h in range(heads):
            m_new = new_m[h]
            alpha = jnp.exp(ms[h] - m_new)
            pr = jnp.exp(s_ref[h] - m_new)
            new_l.append(alpha * ls[h] + jnp.sum(pr, axis=0, keepdims=True))
            vt = vt_ref[j, h * dh:(h + 1) * dh, :]
            acc_ref[h] = alpha * acc_ref[h] + _dot(vt, pr.astype(BF16))
        return tuple(new_m), tuple(new_l)

    init = (tuple(jnp.full((1, blk), -jnp.inf, F32) for _ in range(heads)),
            tuple(jnp.zeros((1, blk), F32) for _ in range(heads)))
    _, l_f = lax.fori_loop(0, i + 1, jbody, init)
    for h in range(heads):
        o_ref[h * dh:(h + 1) * dh, :] = (acc_ref[h] / l_f[h]).astype(BF16)


def _moba(qt, pb3, vt, bias, *, bcols):
    bsz, seq, _ = pb3.shape
    blk = MOBA_BLOCK
    nb = seq // blk
    assert seq % blk == 0
    kb0 = bcols["kb"][0]
    kern = functools.partial(_moba_kernel, nb=nb)
    return pl.pallas_call(
        kern,
        grid=(bsz, nb),
        in_specs=[
            pl.BlockSpec((None, None, MOBA_W, blk), lambda b, i: (b, i, 0, 0)),
            pl.BlockSpec((None, seq, MOBA_W), lambda b, i: (b, 0, kb0 // MOBA_W)),
            pl.BlockSpec((None, nb, MOBA_W, blk), lambda b, i: (b, 0, 0, 0)),
            _resident((MOBA_HEADS, 3, blk, blk), lambda b, i: (0, 0, 0, 0)),
        ],
        out_specs=pl.BlockSpec((None, None, MOBA_W, blk), lambda b, i: (b, i, 0, 0)),
        out_shape=jax.ShapeDtypeStruct((bsz, nb, MOBA_W, blk), BF16),
        scratch_shapes=[
            pltpu.VMEM((MOBA_HEADS, seq, 2 * MOBA_DH), BF16),
            pltpu.VMEM((MOBA_HEADS // 2, nb, 2 * MOBA_DH), F32),
            pltpu.VMEM((MOBA_HEADS, 2 * MOBA_DH, blk), BF16),
            pltpu.VMEM((MOBA_HEADS, MOBA_DH, blk), F32),
            pltpu.VMEM((MOBA_HEADS, blk, blk), F32),
        ],
        compiler_params=_params("arbitrary", "arbitrary"),
        name="moba",
    )(qt, pb3, vt, bias)


def _merge_kernel(oa_ref, obt_ref, g_ref, x_ref, wg_ref, wm_ref, wo_ref, o_ref, *, d):
    ya = _dot(oa_ref[...], wg_ref[...])
    wm = wm_ref[...]
    yb = jnp.concatenate([_dot_tn(obt_ref[s], wm) for s in range(obt_ref.shape[0])], axis=0)
    mixed = _sigmoid(g_ref[:, 0:d]) * ya + _sigmoid(g_ref[:, d:2 * d]) * yb
    o_ref[...] = x_ref[...] + _dot(mixed.astype(BF16), wo_ref[...])


def _merge(oa2, obt, pf2, x2, wg, wm, wo, layer, *, seq, tm):
    n, d = x2.shape
    blk = MOBA_BLOCK
    tpb = seq // tm
    kern = functools.partial(_merge_kernel, d=d)
    return pl.pallas_call(
        kern,
        grid=(n // tm,),
        in_specs=[
            pl.BlockSpec((tm, GLA_V_W), lambda i: (i, 0)),
            pl.BlockSpec((None, tm // blk, MOBA_W, blk), lambda i: (i // tpb, i % tpb, 0, 0)),
            pl.BlockSpec((tm, D_MERGE * d), lambda i: (i, 0)),
            pl.BlockSpec((tm, d), lambda i: (i, 0)),
            _resident((None, GLA_V_W, d), lambda i: (layer, 0, 0)),
            _resident((None, MOBA_W, d), lambda i: (layer, 0, 0)),
            _resident((None, d, d), lambda i: (layer, 0, 0)),
        ],
        out_specs=pl.BlockSpec((tm, d), lambda i: (i, 0)),
        out_shape=jax.ShapeDtypeStruct((n, d), F32),
        compiler_params=_params("parallel"),
        name="merge_out",
    )(oa2, obt, pf2, x2, wg, wm, wo)


HALO = 8


def _ffn_kernel(x_ref, g_ref, wu_ref, cw_ref, cb_ref, wd_ref, gf_ref, o_ref, carry_ref, act_ref,
                *, d_ff, cw, kgroup, final):
    t = pl.program_id(1)
    tm = x_ref.shape[0]

    @pl.when(t == 0)
    def _zero_history():
        carry_ref[...] = jnp.zeros_like(carry_ref)

    x = x_ref[...]
    ms = jnp.mean(x * x, axis=-1, keepdims=True)
    h = (x * lax.rsqrt(ms + EPS) * g_ref[...]).astype(BF16)

    row_h = lax.broadcasted_iota(jnp.int32, (HALO, cw), 0)

    def shifted(u, prev, s):
        rolled = pltpu.roll(u, s, axis=0)
        head = jnp.where(row_h < s, pltpu.roll(prev, s, axis=0), rolled[0:HALO, :])
        return jnp.concatenate([head, rolled[HALO:, :]], axis=0)

    y = x
    nchunks = d_ff // cw
    for c in range(nchunks):
        halves = []
        for part in range(2):
            c0 = part * d_ff + c * cw
            u = _dot(h, wu_ref[:, c0:c0 + cw])
            prev = carry_ref[:, c0:c0 + cw]
            carry_ref[:, c0:c0 + cw] = u[tm - HALO:tm, :]
            w = cw_ref[:, c0:c0 + cw]
            conv = cb_ref[:, c0:c0 + cw] + w[CONV_W - 1:CONV_W, :] * u
            for s in range(1, CONV_W):
                conv = conv + w[CONV_W - 1 - s:CONV_W - s, :] * shifted(u, prev, s)
            halves.append(conv)
        a, bval = halves
        act_ref[:, c * cw:(c + 1) * cw] = ((a * _sigmoid(a)) * bval).astype(BF16)
        if (c + 1) % kgroup == 0 or c == nchunks - 1:
            k0 = (c // kgroup) * kgroup * cw
            k1 = (c + 1) * cw
            y = y + _dot(act_ref[:, k0:k1], wd_ref[k0:k1, :])
    if final:
        ms2 = jnp.mean(y * y, axis=-1, keepdims=True)
        y = y * lax.rsqrt(ms2 + EPS) * gf_ref[...]
    o_ref[...] = y


def _ffn(x3, g, wu, cw_, cb, wd, gfinal, layer, *, tm, final):
    bsz, seq, d = x3.shape
    d_ff = wd.shape[1]
    cw = 256
    assert d_ff % cw == 0 and seq % tm == 0
    kern = functools.partial(_ffn_kernel, d_ff=d_ff, cw=cw, kgroup=4, final=final)
    return pl.pallas_call(
        kern,
        grid=(bsz, seq // tm),
        in_specs=[
            pl.BlockSpec((None, tm, d), lambda b, t: (b, t, 0)),
            pl.BlockSpec((None, 1, d), lambda b, t: (layer, 0, 0)),
            _resident((None, d, 2 * d_ff), lambda b, t: (layer, 0, 0)),
            pl.BlockSpec((None, CONV_W, 2 * d_ff), lambda b, t: (layer, 0, 0)),
            pl.BlockSpec((None, 1, 2 * d_ff), lambda b, t: (layer, 0, 0)),
            _resident((None, d_ff, d), lambda b, t: (layer, 0, 0)),
            pl.BlockSpec((1, d), lambda b, t: (0, 0)),
        ],
        out_specs=pl.BlockSpec((None, tm, d), lambda b, t: (b, t, 0)),
        out_shape=jax.ShapeDtypeStruct((bsz, seq, d), F32),
        scratch_shapes=[
            pltpu.VMEM((HALO, 2 * d_ff), F32),
            pltpu.VMEM((tm, d_ff), BF16),
        ],
        compiler_params=_params("arbitrary", "arbitrary"),
        name="ffn",
    )(x3, g, wu, cw_, cb, wd, gfinal)


def kernel(x, rel_bias, norm_mix, w_in, w_lr_up, b_forget, gla_out_norm, w_branch_gla, w_branch_moba,
           w_out, norm_ffn, w_up, conv_w, conv_b, w_down, norm_final):
    bsz, seq, d = x.shape
    depth = w_in.shape[0]
    n = bsz * seq
    fcols, wf, bcols, wb = _proj_layout(d)

    w_in_r, w_in_t = _regroup_w_in(w_in, d)
    wlr = jnp.pad(w_lr_up, ((0, 0), (0, ALR_PAD - GLA_RANK), (0, 0)))
    wg = w_branch_gla.astype(BF16)
    wm = w_branch_moba.astype(BF16)
    wo = w_out.astype(BF16)
    wu = w_up.astype(BF16)
    wd = w_down.astype(BF16)
    norm_mix3 = norm_mix[:, None, :]
    norm_ffn3 = norm_ffn[:, None, :]
    b_forget3 = b_forget[:, None, :]
    gnorm3 = gla_out_norm[:, None, :]
    conv_b3 = conv_b[:, None, :]
    gfinal = norm_final[None, :]

    bias = _bias_tiles(rel_bias, seq)

    for l in range(depth):
        pf, pb, qt, vt = _in_proj(x.reshape(n, d), norm_mix3, w_in_r, w_in_t, l, seq=seq, wf=wf, wb=wb, tm=512)
        pf3 = pf.reshape(bsz, seq, wf)
        pb3 = pb.reshape(bsz, seq, wb)
        oa = _gla(pf3, pb3, wlr, b_forget3, gnorm3, l, fcols=fcols, bcols=bcols)
        obt = _moba(qt, pb3, vt, bias, bcols=bcols)
        x1 = _merge(oa.reshape(n, GLA_V_W), obt, pf, x.reshape(n, d), wg, wm, wo, l, seq=seq, tm=512)
        x = _ffn(x1.reshape(bsz, seq, d), norm_ffn3, wu, conv_w, conv_b3, wd, gfinal, l,
                 tm=512, final=(l == depth - 1))
    return x
```

```python
import functools
import math

import jax
import jax.numpy as jnp
import numpy as np
from jax import lax
from jax.experimental import pallas as pl
from jax.experimental.pallas import tpu as pltpu

GLA_HEADS = 4
GLA_DK = 128
GLA_DV = 256
GLA_RANK = 16
GLA_TAU = 16.0
GLA_CHUNK = 64
MOBA_HEADS = 8
MOBA_DH = 64
MOBA_BLOCK = 256
MOBA_TOPK = 3
REL_BUCKETS = 32
REL_MAX_DIST = 128
CONV_W = 3
EPS = 1e-6
NEG_INF = -1e30

GLA_QK_W = GLA_HEADS * GLA_DK
GLA_V_W = GLA_HEADS * GLA_DV
MOBA_W = MOBA_HEADS * MOBA_DH

LANES = 128
VMEM_LIMIT_BYTES = 56 * 1024 * 1024

F32 = jnp.float32
BF16 = jnp.bfloat16
HIGHEST = lax.Precision.HIGHEST

_NT = (((1,), (1,)), ((), ()))
_TN = (((0,), (0,)), ((), ()))


def _dot(a, b, precision=None):
    return jnp.dot(a, b, preferred_element_type=F32, precision=precision)


def _dot_nt(a, b, precision=None):
    return lax.dot_general(a, b, _NT, preferred_element_type=F32, precision=precision)


def _dot_tn(a, b):
    return lax.dot_general(a, b, _TN, preferred_element_type=F32)


def _split_bf16(x, n):
    parts = []
    for _ in range(n - 1):
        p = x.astype(BF16)
        parts.append(p)
        x = x - p.astype(F32)
    parts.append(x.astype(BF16))
    return parts


def _sigmoid(x):
    return 1.0 / (1.0 + jnp.exp(-x))


def _params(*semantics):
    return pltpu.CompilerParams(dimension_semantics=semantics, vmem_limit_bytes=VMEM_LIMIT_BYTES)


def _resident(shape, index_map):
    return pl.BlockSpec(shape, index_map, pipeline_mode=pl.Buffered(1))


D_MERGE = 2
ALR_PAD = LANES


def _proj_layout(d_model):
    f32_cols = dict(gates=(0, D_MERGE * d_model))
    off = D_MERGE * d_model
    for name, w in (("ra", GLA_V_W), ("qa", GLA_QK_W), ("ka", GLA_QK_W), ("alr", ALR_PAD)):
        f32_cols[name] = (off, w)
        off += w
    wf = off
    bf_cols = {}
    off = 0
    for name, w in (("va", GLA_V_W), ("kb", MOBA_W)):
        bf_cols[name] = (off, w)
        off += w
    return f32_cols, wf, bf_cols, off


def _split_w_in(w_in, d_model, f32_names, bf16_names):
    splits = (GLA_QK_W, GLA_QK_W, GLA_V_W, GLA_V_W, GLA_RANK, MOBA_W, MOBA_W, MOBA_W, D_MERGE * d_model)
    pts = np.cumsum(splits)[:-1].tolist()
    pieces = dict(zip(("qa", "ka", "va", "ra", "alr", "qb", "kb", "vb", "gates"), jnp.split(w_in, pts, axis=-1)))
    pieces["alr"] = jnp.pad(pieces["alr"], ((0, 0), (0, 0), (0, ALR_PAD - GLA_RANK)))
    rows = [pieces[k].astype(BF16) for k in tuple(f32_names) + tuple(bf16_names)]
    transposed = [jnp.swapaxes(pieces[k], 1, 2).astype(BF16) for k in ("qb", "vb")]
    return rows, transposed


def _in_proj_kernel(x_ref, g_ref, *refs, n_f32, n_bf16, chunk):
    w_f32 = refs[:n_f32]
    w_bf16 = refs[n_f32:n_f32 + n_bf16]
    wqt_ref, wvt_ref, of_ref, ob_ref, qt_ref, vt_ref = refs[n_f32 + n_bf16:]
    x = x_ref[...]
    ms = jnp.mean(x * x, axis=-1, keepdims=True)
    h = (x * lax.rsqrt(ms + EPS) * g_ref[...]).astype(BF16)
    for w_refs, out_ref in ((w_f32, of_ref), (w_bf16, ob_ref)):
        off = 0
        for w_ref in w_refs:
            width = w_ref.shape[1]
            for c0 in range(0, width, chunk):
                c1 = min(c0 + chunk, width)
                out_ref[:, off + c0:off + c1] = _dot(h, w_ref[:, c0:c1]).astype(out_ref.dtype)
            off += width
    blk = MOBA_BLOCK
    for s in range(x.shape[0] // blk):
        hs = h[s * blk:(s + 1) * blk, :]
        qt_ref[s] = _dot_nt(wqt_ref[...], hs).astype(BF16)
        vt_ref[s] = _dot_nt(wvt_ref[...], hs).astype(BF16)


def _in_proj(x2, g, w_rows, w_t, layer, *, seq, n_f32, wf, wb, tm):
    n, d = x2.shape
    blk = MOBA_BLOCK
    tpb = seq // tm
    spt = tm // blk
    kern = functools.partial(_in_proj_kernel, n_f32=n_f32, n_bf16=len(w_rows) - n_f32, chunk=512)
    t_shape = jax.ShapeDtypeStruct((n // seq, seq // blk, MOBA_W, blk), BF16)
    t_spec = pl.BlockSpec((None, spt, MOBA_W, blk), lambda i: (i // tpb, i % tpb, 0, 0))
    w_specs = [_resident((None,) + w.shape[1:], lambda i: (layer, 0, 0)) for w in tuple(w_rows) + tuple(w_t)]
    return pl.pallas_call(
        kern,
        grid=(n // tm,),
        in_specs=[
            pl.BlockSpec((tm, d), lambda i: (i, 0)),
            pl.BlockSpec((None, 1, d), lambda i: (layer, 0, 0)),
            *w_specs,
        ],
        out_specs=[
            pl.BlockSpec((tm, wf), lambda i: (i, 0)),
            pl.BlockSpec((tm, wb), lambda i: (i, 0)),
            t_spec,
            t_spec,
        ],
        out_shape=[jax.ShapeDtypeStruct((n, wf), F32), jax.ShapeDtypeStruct((n, wb), BF16), t_shape, t_shape],
        compiler_params=_params("parallel"),
        name="in_proj",
    )(x2, g, *w_rows, *w_t)


GLA_SUPER = 256


def _gla_kernel(q_ref, k_ref, v_ref, r_ref, alr_ref, wlr_ref, bf_ref, gn_ref, o_ref,
                qd_ref, u_ref, dec_ref, oi_ref, sb_ref, b_ref, kd_ref, ke_ref, at_ref, *, seq):
    c, sb = GLA_CHUNK, GLA_SUPER
    cps = sb // c
    nsb = seq // sb
    nchunks = seq // c
    shift = c.bit_length() - 1
    assert (1 << shift) == c and seq % sb == 0
    row = lax.broadcasted_iota(jnp.int32, (sb, sb), 0)
    col = lax.broadcasted_iota(jnp.int32, (sb, sb), 1)
    same_chunk = (row >> shift) == (col >> shift)
    causal = same_chunk & (row >= col)
    tril = causal.astype(BF16)
    wlr_parts = _split_bf16(wlr_ref[...], 2)
    bfg = bf_ref[...]
    gn = gn_ref[...]
    q_scale = GLA_DK ** -0.5

    blocks = [slice(s * sb, (s + 1) * sb) for s in range(nsb)]

    for rows in blocks:
        a_hi, a_lo = _split_bf16(alr_ref[rows, :], 2)
        xa = _dot(a_hi, wlr_parts[0]) + _dot(a_hi, wlr_parts[1]) + _dot(a_lo, wlr_parts[0]) + bfg
        log_a = (jnp.minimum(xa, 0.0) - jnp.log1p(jnp.exp(-jnp.abs(xa)))) * (1.0 / GLA_TAU)
        pieces = _dot(tril, jnp.concatenate(_split_bf16(log_a, 3), axis=1))
        b_ref[rows, :] = (pieces[:, 0:GLA_DK] + pieces[:, GLA_DK:2 * GLA_DK]) + pieces[:, 2 * GLA_DK:3 * GLA_DK]

    for s, rows in enumerate(blocks):
        b = b_ref[rows, :]
        b_last = jnp.concatenate(
            [jnp.broadcast_to(b[(cc + 1) * c - 1:(cc + 1) * c, :], (c, GLA_DK)) for cc in range(cps)], axis=0)
        k = k_ref[rows, :]
        qd_ref[rows, :] = ((q_ref[rows, :] * q_scale) * jnp.exp(b)).astype(BF16)
        kd_ref[rows, :] = (k * jnp.exp(-b)).astype(BF16)
        ke_ref[rows, :] = (k * jnp.exp(b_last - b)).astype(BF16)
        chunk_decay = jnp.exp(b_last)
        for cc in range(cps):
            dec_ref[s * cps + cc] = chunk_decay[cc * c:cc * c + 8, :]

    for rows in blocks:
        at_ref[rows, :] = jnp.where(causal, _dot_nt(qd_ref[rows, :], kd_ref[rows, :]), 0.0).astype(BF16)

    for n in range(nchunks):
        u_ref[n] = _dot_tn(v_ref[n * c:(n + 1) * c, :], ke_ref[n * c:(n + 1) * c, :])

    for rows in blocks:
        oi_ref[rows, :] = _dot(at_ref[rows, :], v_ref[rows, :])

    def scan(n, st):
        sb_ref[n] = st.astype(BF16)
        return st * dec_ref[n, 0:1, :] + u_ref[n]

    lax.fori_loop(0, nchunks, scan, jnp.zeros((GLA_DV, GLA_DK), F32), unroll=4)

    for n in range(nchunks):
        crow = slice(n * c, (n + 1) * c)
        oi_ref[crow, :] = oi_ref[crow, :] + _dot_nt(qd_ref[crow, :], sb_ref[n])

    for rows in blocks:
        o = oi_ref[rows, :]
        ms = jnp.mean(o * o, axis=-1, keepdims=True)
        r = r_ref[rows, :]
        out = (o * lax.rsqrt(ms + EPS) * gn) * (r * _sigmoid(r))
        o_ref[rows, :] = out.astype(BF16)


def _gla(pf3, pb3, wlr, bforget, gnorm, layer, *, fcols, bcols):
    bsz, seq, _ = pf3.shape
    qa0, ka0, ra0, alr0 = (fcols[k][0] for k in ("qa", "ka", "ra", "alr"))
    va0 = bcols["va"][0]
    kern = functools.partial(_gla_kernel, seq=seq)
    return pl.pallas_call(
        kern,
        grid=(bsz, GLA_HEADS),
        in_specs=[
            pl.BlockSpec((None, seq, GLA_DK), lambda b, h: (b, 0, qa0 // GLA_DK + h)),
            pl.BlockSpec((None, seq, GLA_DK), lambda b, h: (b, 0, ka0 // GLA_DK + h)),
            pl.BlockSpec((None, seq, GLA_DV), lambda b, h: (b, 0, va0 // GLA_DV + h)),
            pl.BlockSpec((None, seq, GLA_DV), lambda b, h: (b, 0, ra0 // GLA_DV + h)),
            pl.BlockSpec((None, seq, ALR_PAD), lambda b, h: (b, 0, alr0 // ALR_PAD)),
            pl.BlockSpec((None, ALR_PAD, GLA_DK), lambda b, h: (layer, 0, h)),
            pl.BlockSpec((None, 1, GLA_DK), lambda b, h: (layer, 0, h)),
            pl.BlockSpec((None, 1, GLA_DV), lambda b, h: (layer, 0, h)),
        ],
        out_specs=pl.BlockSpec((None, seq, GLA_DV), lambda b, h: (b, 0, h)),
        out_shape=jax.ShapeDtypeStruct((bsz, seq, GLA_V_W), BF16),
        scratch_shapes=[
            pltpu.VMEM((seq, GLA_DK), BF16),
            pltpu.VMEM((seq // GLA_CHUNK, GLA_DV, GLA_DK), F32),
            pltpu.VMEM((seq // GLA_CHUNK, 8, GLA_DK), F32),
            pltpu.VMEM((seq, GLA_DV), F32),
            pltpu.VMEM((seq // GLA_CHUNK, GLA_DV, GLA_DK), BF16),
            pltpu.VMEM((seq, GLA_DK), F32),
            pltpu.VMEM((seq, GLA_DK), BF16),
            pltpu.VMEM((seq, GLA_DK), BF16),
            pltpu.VMEM((seq, GLA_SUPER), BF16),
        ],
        compiler_params=_params("parallel", "parallel"),
        name="gla",
    )(pf3, pf3, pb3, pf3, pf3, wlr, bforget, gnorm)


def _t5_bucket(rel):
    n = jnp.maximum(rel, 0)
    max_exact = REL_BUCKETS // 2
    nf = jnp.maximum(n, 1).astype(F32)
    large = max_exact + (jnp.log(nf / max_exact) / math.log(REL_MAX_DIST / max_exact)
                         * (REL_BUCKETS - max_exact)).astype(jnp.int32)
    large = jnp.minimum(large, REL_BUCKETS - 1)
    return jnp.where(n < max_exact, n, large)


def _far_bucket(seq):
    d = np.arange(MOBA_BLOCK + 1, max(seq, MOBA_BLOCK + 2), dtype=np.float64)
    max_exact = REL_BUCKETS // 2
    large = max_exact + np.floor(np.log(d / max_exact) / math.log(REL_MAX_DIST / max_exact)
                                 * (REL_BUCKETS - max_exact) + 1e-6)
    assert MOBA_BLOCK + 1 >= max_exact and np.all(large >= REL_BUCKETS - 1), "far blocks must share one bucket"
    return REL_BUCKETS - 1


def _bias_kernel(bucket_ref, rb_ref, o_ref, *, far_bucket):
    h = pl.program_id(0)
    blk = MOBA_BLOCK
    far = rb_ref[far_bucket, h]
    row = lax.broadcasted_iota(jnp.int32, (blk, blk), 0)
    col = lax.broadcasted_iota(jnp.int32, (blk, blk), 1)
    for t in range(2):
        bk = bucket_ref[t]
        acc = jnp.zeros((blk, blk), F32)
        for b in range(REL_BUCKETS):
            acc = jnp.where(bk == b, rb_ref[b, h], acc)
        acc = acc - far
        if t == 1:
            acc = jnp.where(col >= row, acc, NEG_INF)
        o_ref[t] = acc


def _bias_tiles(rel_bias, seq):
    blk = MOBA_BLOCK
    kpos = jnp.arange(blk, dtype=jnp.int32)[:, None]
    qpos = jnp.arange(blk, dtype=jnp.int32)[None, :]
    bucket = jnp.stack([_t5_bucket(qpos + blk - kpos), _t5_bucket(qpos - kpos)])
    kern = functools.partial(_bias_kernel, far_bucket=_far_bucket(seq))
    return pl.pallas_call(
        kern,
        grid=(MOBA_HEADS,),
        in_specs=[
            pl.BlockSpec((2, blk, blk), lambda h: (0, 0, 0)),
            pl.BlockSpec(memory_space=pltpu.SMEM),
        ],
        out_specs=pl.BlockSpec((None, 2, blk, blk), lambda h: (h, 0, 0, 0)),
        out_shape=jax.ShapeDtypeStruct((MOBA_HEADS, 2, blk, blk), F32),
        compiler_params=_params("arbitrary"),
        name="rel_bias_tiles",
    )(bucket, rel_bias)


PEN_ROWS = 16


def _moba_kernel(qt_ref, k_ref, vt_ref, bias_ref, o_ref, kaug_ref, kmean_ref, qaug_ref, acc_ref, s_ref, *, nb):
    i = pl.program_id(1)
    blk, dh, heads = MOBA_BLOCK, MOBA_DH, MOBA_HEADS
    seq = nb * blk
    pair_w = 2 * dh
    assert pair_w == LANES and nb <= PEN_ROWS <= dh
    blk_shift = blk.bit_length() - 1
    assert (1 << blk_shift) == blk
    lane = lax.broadcasted_iota(jnp.int32, (1, pair_w), 1)

    @pl.when(i == 0)
    def _build_keys():
        rowblk = lax.broadcasted_iota(jnp.int32, (seq, pair_w), 0) >> blk_shift
        lane_s = lax.broadcasted_iota(jnp.int32, (seq, pair_w), 1)
        for p in range(heads // 2):
            kp = k_ref[:, p * pair_w:(p + 1) * pair_w]
            km = jnp.sum(kp.astype(F32).reshape(nb, blk, pair_w), axis=1) * (1.0 / blk)
            kmean_ref[p] = km
            for e in range(2):
                own = (lane_s >= dh) if e else (lane_s < dh)
                onehot = (lane_s - (0 if e else dh)) == rowblk
                kaug_ref[2 * p + e] = jnp.where(own, kp, onehot.astype(BF16))

    row8 = lax.broadcasted_iota(jnp.int32, (nb, blk), 0)
    scale = jnp.asarray(dh ** -0.5, BF16)

    for p in range(heads // 2):
        qtp = qt_ref[p * pair_w:(p + 1) * pair_w, :]
        km = kmean_ref[p]
        for e in range(2):
            own = (lane >= dh) if e else (lane < dh)
            kme = jnp.where(own, km, 0.0)
            km_hi = kme.astype(BF16)
            km_lo = (kme - km_hi.astype(F32)).astype(BF16)
            g = _dot(km_hi, qtp) + _dot(km_lo, qtp)
            g = jnp.where(row8 < i, g, -jnp.inf)
            rank = jnp.zeros((nb, blk), jnp.int32)
            for m in range(nb):
                gm = g[m:m + 1, :]
                beats = (gm > g) | ((gm == g) & (m < row8))
                rank = rank + beats.astype(jnp.int32)
            keep = ((rank < MOBA_TOPK) & (row8 < i)) | (row8 == i)
            pen_t = jnp.where(keep, 0.0, NEG_INF)
            pen = jnp.concatenate([pen_t, jnp.zeros((PEN_ROWS - nb, blk), F32)], axis=0).astype(BF16)
            qs = qtp[e * dh:(e + 1) * dh, :] * scale
            if e == 0:
                parts = [qs, pen, jnp.zeros((pair_w - dh - PEN_ROWS, blk), BF16)]
            else:
                parts = [pen, jnp.zeros((dh - PEN_ROWS, blk), BF16), qs]
            qaug_ref[2 * p + e] = jnp.concatenate(parts, axis=0)

    def scores(j, nblocks, slot):
        col_max = []
        for h in range(heads):
            s = _dot(kaug_ref[h, j * blk:(j + 1) * blk, :], qaug_ref[h])
            if j >= nblocks - 2:
                s = s + bias_ref[h, j - (nblocks - 2)]
            s_ref[slot, h] = s
            col_max.append(jnp.max(s, axis=0, keepdims=True))
        return col_max

    def attend(nblocks):
        col_max = scores(0, nblocks, 0)
        ms = [None] * heads
        ls = [None] * heads
        for j in range(nblocks):
            next_max = scores(j + 1, nblocks, (j + 1) % 2) if j + 1 < nblocks else None
            for h in range(heads):
                m_new = col_max[h] if j == 0 else jnp.maximum(ms[h], col_max[h])
                pr = jnp.exp(s_ref[j % 2, h] - m_new)
                psum = jnp.sum(pr, axis=0, keepdims=True)
                pv = _dot(vt_ref[j, h * dh:(h + 1) * dh, :], pr.astype(BF16))
                if j == 0:
                    ls[h] = psum
                    acc_ref[h] = pv
                else:
                    alpha = jnp.exp(ms[h] - m_new)
                    ls[h] = alpha * ls[h] + psum
                    acc_ref[h] = alpha * acc_ref[h] + pv
                ms[h] = m_new
            col_max = next_max
        for h in range(heads):
            o_ref[h * dh:(h + 1) * dh, :] = (acc_ref[h] / ls[h]).astype(BF16)

    for nblocks in range(1, nb + 1):
        pl.when(i == nblocks - 1)(functools.partial(attend, nblocks))


def _moba(qt, pb3, vt, bias, *, bcols):
    bsz, seq, _ = pb3.shape
    blk = MOBA_BLOCK
    nb = seq // blk
    assert seq % blk == 0
    kb0 = bcols["kb"][0]
    kern = functools.partial(_moba_kernel, nb=nb)
    return pl.pallas_call(
        kern,
        grid=(bsz, nb),
        in_specs=[
            pl.BlockSpec((None, None, MOBA_W, blk), lambda b, i: (b, i, 0, 0)),
            pl.BlockSpec((None, seq, MOBA_W), lambda b, i: (b, 0, kb0 // MOBA_W)),
            pl.BlockSpec((None, nb, MOBA_W, blk), lambda b, i: (b, 0, 0, 0)),
            _resident((MOBA_HEADS, 2, blk, blk), lambda b, i: (0, 0, 0, 0)),
        ],
        out_specs=pl.BlockSpec((None, None, MOBA_W, blk), lambda b, i: (b, i, 0, 0)),
        out_shape=jax.ShapeDtypeStruct((bsz, nb, MOBA_W, blk), BF16),
        scratch_shapes=[
            pltpu.VMEM((MOBA_HEADS, seq, 2 * MOBA_DH), BF16),
            pltpu.VMEM((MOBA_HEADS // 2, nb, 2 * MOBA_DH), F32),
            pltpu.VMEM((MOBA_HEADS, 2 * MOBA_DH, blk), BF16),
            pltpu.VMEM((MOBA_HEADS, MOBA_DH, blk), F32),
            pltpu.VMEM((2, MOBA_HEADS, blk, blk), F32),
        ],
        compiler_params=_params("arbitrary", "arbitrary"),
        name="moba",
    )(qt, pb3, vt, bias)


def _merge_kernel(oa_ref, obt_ref, g_ref, x_ref, wg_ref, wm_ref, wo_ref, o_ref, *, d):
    ya = _dot(oa_ref[...], wg_ref[...])
    wm = wm_ref[...]
    yb = jnp.concatenate([_dot_tn(obt_ref[s], wm) for s in range(obt_ref.shape[0])], axis=0)
    mixed = _sigmoid(g_ref[:, 0:d]) * ya + _sigmoid(g_ref[:, d:2 * d]) * yb
    o_ref[...] = x_ref[...] + _dot(mixed.astype(BF16), wo_ref[...])


def _merge(oa2, obt, pf2, x2, wg, wm, wo, layer, *, seq, tm):
    n, d = x2.shape
    blk = MOBA_BLOCK
    tpb = seq // tm
    kern = functools.partial(_merge_kernel, d=d)
    return pl.pallas_call(
        kern,
        grid=(n // tm,),
        in_specs=[
            pl.BlockSpec((tm, GLA_V_W), lambda i: (i, 0)),
            pl.BlockSpec((None, tm // blk, MOBA_W, blk), lambda i: (i // tpb, i % tpb, 0, 0)),
            pl.BlockSpec((tm, D_MERGE * d), lambda i: (i, 0)),
            pl.BlockSpec((tm, d), lambda i: (i, 0)),
            _resident((None, GLA_V_W, d), lambda i: (layer, 0, 0)),
            _resident((None, MOBA_W, d), lambda i: (layer, 0, 0)),
            _resident((None, d, d), lambda i: (layer, 0, 0)),
        ],
        out_specs=pl.BlockSpec((tm, d), lambda i: (i, 0)),
        out_shape=jax.ShapeDtypeStruct((n, d), F32),
        compiler_params=_params("parallel"),
        name="merge_out",
    )(oa2, obt, pf2, x2, wg, wm, wo)


HALO = 8


def _ffn_kernel(x_ref, g_ref, wu_ref, cw_ref, cb_ref, wd_ref, gf_ref, o_ref, carry_ref, act_ref,
                *, d_ff, cw, kgroup, final):
    t = pl.program_id(1)
    tm = x_ref.shape[0]

    @pl.when(t == 0)
    def _zero_history():
        carry_ref[...] = jnp.zeros_like(carry_ref)

    x = x_ref[...]
    ms = jnp.mean(x * x, axis=-1, keepdims=True)
    h = (x * lax.rsqrt(ms + EPS) * g_ref[...]).astype(BF16)

    row_h = lax.broadcasted_iota(jnp.int32, (HALO, cw), 0)

    def shifted(u, prev, s):
        rolled = pltpu.roll(u, s, axis=0)
        head = jnp.where(row_h < s, pltpu.roll(prev, s, axis=0), rolled[0:HALO, :])
        return jnp.concatenate([head, rolled[HALO:, :]], axis=0)

    y = x
    nchunks = d_ff // cw
    for c in range(nchunks):
        halves = []
        for part in range(2):
            c0 = part * d_ff + c * cw
            u = _dot(h, wu_ref[:, c0:c0 + cw])
            prev = carry_ref[:, c0:c0 + cw]
            carry_ref[:, c0:c0 + cw] = u[tm - HALO:tm, :]
            w = cw_ref[:, c0:c0 + cw]
            conv = cb_ref[:, c0:c0 + cw] + w[CONV_W - 1:CONV_W, :] * u
            for s in range(1, CONV_W):
                conv = conv + w[CONV_W - 1 - s:CONV_W - s, :] * shifted(u, prev, s)
            halves.append(conv)
        a, bval = halves
        act_ref[:, c * cw:(c + 1) * cw] = ((a * _sigmoid(a)) * bval).astype(BF16)
        if (c + 1) % kgroup == 0 or c == nchunks - 1:
            k0 = (c // kgroup) * kgroup * cw
            k1 = (c + 1) * cw
            y = y + _dot(act_ref[:, k0:k1], wd_ref[k0:k1, :])
    if final:
        ms2 = jnp.mean(y * y, axis=-1, keepdims=True)
        y = y * lax.rsqrt(ms2 + EPS) * gf_ref[...]
    o_ref[...] = y


def _ffn(x3, g, wu, cw_, cb, wd, gfinal, layer, *, tm, final):
    bsz, seq, d = x3.shape
    d_ff = wd.shape[1]
    cw = 256
    assert d_ff % cw == 0 and seq % tm == 0
    kern = functools.partial(_ffn_kernel, d_ff=d_ff, cw=cw, kgroup=4, final=final)
    return pl.pallas_call(
        kern,
        grid=(bsz, seq // tm),
        in_specs=[
            pl.BlockSpec((None, tm, d), lambda b, t: (b, t, 0)),
            pl.BlockSpec((None, 1, d), lambda b, t: (layer, 0, 0)),
            _resident((None, d, 2 * d_ff), lambda b, t: (layer, 0, 0)),
            pl.BlockSpec((None, CONV_W, 2 * d_ff), lambda b, t: (layer, 0, 0)),
            pl.BlockSpec((None, 1, 2 * d_ff), lambda b, t: (layer, 0, 0)),
            _resident((None, d_ff, d), lambda b, t: (layer, 0, 0)),
            pl.BlockSpec((1, d), lambda b, t: (0, 0)),
        ],
        out_specs=pl.BlockSpec((None, tm, d), lambda b, t: (b, t, 0)),
        out_shape=jax.ShapeDtypeStruct((bsz, seq, d), F32),
        scratch_shapes=[
            pltpu.VMEM((HALO, 2 * d_ff), F32),
            pltpu.VMEM((tm, d_ff), BF16),
        ],
        compiler_params=_params("arbitrary", "arbitrary"),
        name="ffn",
    )(x3, g, wu, cw_, cb, wd, gfinal)


def kernel(x, rel_bias, norm_mix, w_in, w_lr_up, b_forget, gla_out_norm, w_branch_gla, w_branch_moba,
           w_out, norm_ffn, w_up, conv_w, conv_b, w_down, norm_final):
    bsz, seq, d = x.shape
    depth = w_in.shape[0]
    n = bsz * seq
    fcols, wf, bcols, wb = _proj_layout(d)

    w_in_rows, w_in_t = _split_w_in(w_in, d, fcols, bcols)
    wlr = jnp.pad(w_lr_up, ((0, 0), (0, ALR_PAD - GLA_RANK), (0, 0)))
    wg = w_branch_gla.astype(BF16)
    wm = w_branch_moba.astype(BF16)
    wo = w_out.astype(BF16)
    wu = w_up.astype(BF16)
    wd = w_down.astype(BF16)
    norm_mix3 = norm_mix[:, None, :]
    norm_ffn3 = norm_ffn[:, None, :]
    b_forget3 = b_forget[:, None, :]
    gnorm3 = gla_out_norm[:, None, :]
    conv_b3 = conv_b[:, None, :]
    gfinal = norm_final[None, :]

    bias = _bias_tiles(rel_bias, seq)

    for l in range(depth):
        pf, pb, qt, vt = _in_proj(x.reshape(n, d), norm_mix3, w_in_rows, w_in_t, l,
                                  seq=seq, n_f32=len(fcols), wf=wf, wb=wb, tm=512)
        pf3 = pf.reshape(bsz, seq, wf)
        pb3 = pb.reshape(bsz, seq, wb)
        oa = _gla(pf3, pb3, wlr, b_forget3, gnorm3, l, fcols=fcols, bcols=bcols)
        obt = _moba(qt, pb3, vt, bias, bcols=bcols)
        x1 = _merge(oa.reshape(n, GLA_V_W), obt, pf, x.reshape(n, d), wg, wm, wo, l, seq=seq, tm=512)
        x = _ffn(x1.reshape(bsz, seq, d), norm_ffn3, wu, conv_w, conv_b3, wd, gfinal, l,
                 tm=512, final=(l == depth - 1))
    return x
```

```python
import functools
import math

import jax
import jax.numpy as jnp
import numpy as np
from jax import lax
from jax.experimental import pallas as pl
from jax.experimental.pallas import tpu as pltpu

GLA_HEADS = 4
GLA_DK = 128
GLA_DV = 256
GLA_RANK = 16
GLA_TAU = 16.0
GLA_CHUNK = 64
MOBA_HEADS = 8
MOBA_DH = 64
MOBA_BLOCK = 256
MOBA_TOPK = 3
REL_BUCKETS = 32
REL_MAX_DIST = 128
CONV_W = 3
EPS = 1e-6
NEG_INF = -1e30

GLA_QK_W = GLA_HEADS * GLA_DK
GLA_V_W = GLA_HEADS * GLA_DV
MOBA_W = MOBA_HEADS * MOBA_DH

LANES = 128
VMEM_LIMIT_BYTES = 56 * 1024 * 1024

F32 = jnp.float32
BF16 = jnp.bfloat16
HIGHEST = lax.Precision.HIGHEST

_NT = (((1,), (1,)), ((), ()))
_TN = (((0,), (0,)), ((), ()))


def _dot(a, b, precision=None):
    return jnp.dot(a, b, preferred_element_type=F32, precision=precision)


def _dot_nt(a, b, precision=None):
    return lax.dot_general(a, b, _NT, preferred_element_type=F32, precision=precision)


def _dot_tn(a, b):
    return lax.dot_general(a, b, _TN, preferred_element_type=F32)


def _split_bf16(x, n):
    parts = []
    for _ in range(n - 1):
        p = x.astype(BF16)
        parts.append(p)
        x = x - p.astype(F32)
    parts.append(x.astype(BF16))
    return parts


def _sigmoid(x):
    return 1.0 / (1.0 + jnp.exp(-x))


def _params(*semantics):
    return pltpu.CompilerParams(dimension_semantics=semantics, vmem_limit_bytes=VMEM_LIMIT_BYTES)


def _resident(shape, index_map):
    return pl.BlockSpec(shape, index_map, pipeline_mode=pl.Buffered(1))


D_MERGE = 2
ALR_PAD = LANES


def _proj_layout(d_model):
    f32_cols = dict(gates=(0, D_MERGE * d_model))
    off = D_MERGE * d_model
    for name, w in (("ra", GLA_V_W), ("qa", GLA_QK_W), ("ka", GLA_QK_W), ("alr", ALR_PAD)):
        f32_cols[name] = (off, w)
        off += w
    wf = off
    bf_cols = {}
    off = 0
    for name, w in (("va", GLA_V_W), ("kb", MOBA_W)):
        bf_cols[name] = (off, w)
        off += w
    return f32_cols, wf, bf_cols, off


def _split_w_in(w_in, d_model, f32_names, bf16_names):
    splits = (GLA_QK_W, GLA_QK_W, GLA_V_W, GLA_V_W, GLA_RANK, MOBA_W, MOBA_W, MOBA_W, D_MERGE * d_model)
    pts = np.cumsum(splits)[:-1].tolist()
    pieces = dict(zip(("qa", "ka", "va", "ra", "alr", "qb", "kb", "vb", "gates"), jnp.split(w_in, pts, axis=-1)))
    pieces["alr"] = jnp.pad(pieces["alr"], ((0, 0), (0, 0), (0, ALR_PAD - GLA_RANK)))
    rows = [pieces[k].astype(BF16) for k in tuple(f32_names) + tuple(bf16_names)]
    transposed = [jnp.swapaxes(pieces[k], 1, 2).astype(BF16) for k in ("qb", "vb")]
    return rows, transposed


def _in_proj_kernel(x_ref, g_ref, *refs, n_f32, n_bf16, chunk):
    w_f32 = refs[:n_f32]
    w_bf16 = refs[n_f32:n_f32 + n_bf16]
    wqt_ref, wvt_ref, of_ref, ob_ref, qt_ref, vt_ref = refs[n_f32 + n_bf16:]
    x = x_ref[...]
    ms = jnp.mean(x * x, axis=-1, keepdims=True)
    h = (x * lax.rsqrt(ms + EPS) * g_ref[...]).astype(BF16)
    for w_refs, out_ref in ((w_f32, of_ref), (w_bf16, ob_ref)):
        off = 0
        for w_ref in w_refs:
            width = w_ref.shape[1]
            for c0 in range(0, width, chunk):
                c1 = min(c0 + chunk, width)
                out_ref[:, off + c0:off + c1] = _dot(h, w_ref[:, c0:c1]).astype(out_ref.dtype)
            off += width
    blk = MOBA_BLOCK
    for s in range(x.shape[0] // blk):
        hs = h[s * blk:(s + 1) * blk, :]
        qt_ref[s] = _dot_nt(wqt_ref[...], hs).astype(BF16)
        vt_ref[s] = _dot_nt(wvt_ref[...], hs).astype(BF16)


def _in_proj(x2, g, w_rows, w_t, layer, *, seq, n_f32, wf, wb, tm):
    n, d = x2.shape
    blk = MOBA_BLOCK
    tpb = seq // tm
    spt = tm // blk
    kern = functools.partial(_in_proj_kernel, n_f32=n_f32, n_bf16=len(w_rows) - n_f32, chunk=512)
    t_shape = jax.ShapeDtypeStruct((n // seq, seq // blk, MOBA_W, blk), BF16)
    t_spec = pl.BlockSpec((None, spt, MOBA_W, blk), lambda i: (i // tpb, i % tpb, 0, 0))
    w_specs = [_resident((None,) + w.shape[1:], lambda i: (layer, 0, 0)) for w in tuple(w_rows) + tuple(w_t)]
    return pl.pallas_call(
        kern,
        grid=(n // tm,),
        in_specs=[
            pl.BlockSpec((tm, d), lambda i: (i, 0)),
            pl.BlockSpec((None, 1, d), lambda i: (layer, 0, 0)),
            *w_specs,
        ],
        out_specs=[
            pl.BlockSpec((tm, wf), lambda i: (i, 0)),
            pl.BlockSpec((tm, wb), lambda i: (i, 0)),
            t_spec,
            t_spec,
        ],
        out_shape=[jax.ShapeDtypeStruct((n, wf), F32), jax.ShapeDtypeStruct((n, wb), BF16), t_shape, t_shape],
        compiler_params=_params("parallel"),
        name="in_proj",
    )(x2, g, *w_rows, *w_t)


GLA_SUPER = 256
GLA_PARTS = 4


def _gla_kernel(q_ref, k_ref, v_ref, r_ref, alr_ref, wlr_ref, bf_ref, gn_ref, o_ref,
                qd_ref, u_ref, dec_ref, oi_ref, sb_ref, b_ref, kd_ref, ke_ref, at_ref, la_ref, *, seq):
    c, sb = GLA_CHUNK, GLA_SUPER
    cps = sb // c
    nsb = seq // sb
    nchunks = seq // c
    shift = c.bit_length() - 1
    assert (1 << shift) == c and seq % sb == 0
    row = lax.broadcasted_iota(jnp.int32, (sb, sb), 0)
    col = lax.broadcasted_iota(jnp.int32, (sb, sb), 1)
    same_chunk = (row >> shift) == (col >> shift)
    causal = same_chunk & (row >= col)
    tril = causal.astype(BF16)
    wlr_parts = _split_bf16(wlr_ref[...], 2)
    bfg = bf_ref[...]
    gn = gn_ref[...]
    q_scale = GLA_DK ** -0.5

    blocks = [slice(s * sb, (s + 1) * sb) for s in range(nsb)]

    for rows in blocks:
        a_hi, a_lo = _split_bf16(alr_ref[rows, :], 2)
        xa = _dot(a_hi, wlr_parts[0]) + _dot(a_hi, wlr_parts[1]) + _dot(a_lo, wlr_parts[0]) + bfg
        log_a = (jnp.minimum(xa, 0.0) - jnp.log1p(jnp.exp(-jnp.abs(xa)))) * (1.0 / GLA_TAU)
        la_ref[rows, :] = jnp.concatenate(_split_bf16(log_a, 3), axis=1)

    for rows in blocks:
        pieces = _dot(tril, la_ref[rows, :])
        b_ref[rows, :] = (pieces[:, 0:GLA_DK] + pieces[:, GLA_DK:2 * GLA_DK]) + pieces[:, 2 * GLA_DK:3 * GLA_DK]

    for s, rows in enumerate(blocks):
        b = b_ref[rows, :]
        b_last = jnp.concatenate(
            [jnp.broadcast_to(b[(cc + 1) * c - 1:(cc + 1) * c, :], (c, GLA_DK)) for cc in range(cps)], axis=0)
        k = k_ref[rows, :]
        qd_ref[rows, :] = ((q_ref[rows, :] * q_scale) * jnp.exp(b)).astype(BF16)
        kd_ref[rows, :] = (k * jnp.exp(-b)).astype(BF16)
        ke_ref[rows, :] = (k * jnp.exp(b_last - b)).astype(BF16)
        chunk_decay = jnp.exp(b_last)
        for cc in range(cps):
            dec_ref[s * cps + cc] = chunk_decay[cc * c:cc * c + 8, :]

    def intra(part):
        for rows in part:
            at_ref[rows, :] = jnp.where(causal, _dot_nt(qd_ref[rows, :], kd_ref[rows, :]), 0.0).astype(BF16)
        for rows in part:
            for n in range(rows.start // c, rows.stop // c):
                u_ref[n] = _dot_tn(v_ref[n * c:(n + 1) * c, :], ke_ref[n * c:(n + 1) * c, :])
        for rows in part:
            oi_ref[rows, :] = _dot(at_ref[rows, :], v_ref[rows, :])

    def scan(part, st):
        for n in range(part[0].start // c, part[-1].stop // c):
            sb_ref[n] = st.astype(BF16)
            st = st * dec_ref[n, 0:1, :] + u_ref[n]
        return st

    def finish(part):
        for rows in part:
            for n in range(rows.start // c, rows.stop // c):
                crow = slice(n * c, (n + 1) * c)
                oi_ref[crow, :] = oi_ref[crow, :] + _dot_nt(qd_ref[crow, :], sb_ref[n])
        for rows in part:
            o = oi_ref[rows, :]
            ms = jnp.mean(o * o, axis=-1, keepdims=True)
            r = r_ref[rows, :]
            out = (o * lax.rsqrt(ms + EPS) * gn) * (r * _sigmoid(r))
            o_ref[rows, :] = out.astype(BF16)

    per_part = nsb // GLA_PARTS
    parts = [blocks[p * per_part:(p + 1) * per_part] for p in range(GLA_PARTS)]
    state = jnp.zeros((GLA_DV, GLA_DK), F32)
    intra(parts[0])
    for p in range(GLA_PARTS):
        state = scan(parts[p], state)
        if p + 1 < GLA_PARTS:
            intra(parts[p + 1])
        finish(parts[p])


def _gla(pf3, pb3, wlr, bforget, gnorm, layer, *, fcols, bcols):
    bsz, seq, _ = pf3.shape
    qa0, ka0, ra0, alr0 = (fcols[k][0] for k in ("qa", "ka", "ra", "alr"))
    va0 = bcols["va"][0]
    kern = functools.partial(_gla_kernel, seq=seq)
    return pl.pallas_call(
        kern,
        grid=(bsz, GLA_HEADS),
        in_specs=[
            pl.BlockSpec((None, seq, GLA_DK), lambda b, h: (b, 0, qa0 // GLA_DK + h)),
            pl.BlockSpec((None, seq, GLA_DK), lambda b, h: (b, 0, ka0 // GLA_DK + h)),
            pl.BlockSpec((None, seq, GLA_DV), lambda b, h: (b, 0, va0 // GLA_DV + h)),
            pl.BlockSpec((None, seq, GLA_DV), lambda b, h: (b, 0, ra0 // GLA_DV + h)),
            pl.BlockSpec((None, seq, ALR_PAD), lambda b, h: (b, 0, alr0 // ALR_PAD)),
            pl.BlockSpec((None, ALR_PAD, GLA_DK), lambda b, h: (layer, 0, h)),
            pl.BlockSpec((None, 1, GLA_DK), lambda b, h: (layer, 0, h)),
            pl.BlockSpec((None, 1, GLA_DV), lambda b, h: (layer, 0, h)),
        ],
        out_specs=pl.BlockSpec((None, seq, GLA_DV), lambda b, h: (b, 0, h)),
        out_shape=jax.ShapeDtypeStruct((bsz, seq, GLA_V_W), BF16),
        scratch_shapes=[
            pltpu.VMEM((seq, GLA_DK), BF16),
            pltpu.VMEM((seq // GLA_CHUNK, GLA_DV, GLA_DK), F32),
            pltpu.VMEM((seq // GLA_CHUNK, 8, GLA_DK), F32),
            pltpu.VMEM((seq, GLA_DV), F32),
            pltpu.VMEM((seq // GLA_CHUNK, GLA_DV, GLA_DK), BF16),
            pltpu.VMEM((seq, GLA_DK), F32),
            pltpu.VMEM((seq, GLA_DK), BF16),
            pltpu.VMEM((seq, GLA_DK), BF16),
            pltpu.VMEM((seq, GLA_SUPER), BF16),
            pltpu.VMEM((seq, 3 * GLA_DK), BF16),
        ],
        compiler_params=_params("parallel", "parallel"),
        name="gla",
    )(pf3, pf3, pb3, pf3, pf3, wlr, bforget, gnorm)


def _t5_bucket(rel):
    n = jnp.maximum(rel, 0)
    max_exact = REL_BUCKETS // 2
    nf = jnp.maximum(n, 1).astype(F32)
    large = max_exact + (jnp.log(nf / max_exact) / math.log(REL_MAX_DIST / max_exact)
                         * (REL_BUCKETS - max_exact)).astype(jnp.int32)
    large = jnp.minimum(large, REL_BUCKETS - 1)
    return jnp.where(n < max_exact, n, large)


def _far_bucket(seq):
    d = np.arange(MOBA_BLOCK + 1, max(seq, MOBA_BLOCK + 2), dtype=np.float64)
    max_exact = REL_BUCKETS // 2
    large = max_exact + np.floor(np.log(d / max_exact) / math.log(REL_MAX_DIST / max_exact)
                                 * (REL_BUCKETS - max_exact) + 1e-6)
    assert MOBA_BLOCK + 1 >= max_exact and np.all(large >= REL_BUCKETS - 1), "far blocks must share one bucket"
    return REL_BUCKETS - 1


def _bias_kernel(bucket_ref, rb_ref, o_ref, *, far_bucket):
    h = pl.program_id(0)
    blk = MOBA_BLOCK
    far = rb_ref[far_bucket, h]
    row = lax.broadcasted_iota(jnp.int32, (blk, blk), 0)
    col = lax.broadcasted_iota(jnp.int32, (blk, blk), 1)
    for t in range(2):
        bk = bucket_ref[t]
        acc = jnp.zeros((blk, blk), F32)
        for b in range(REL_BUCKETS):
            acc = jnp.where(bk == b, rb_ref[b, h], acc)
        acc = acc - far
        if t == 1:
            acc = jnp.where(col >= row, acc, NEG_INF)
        o_ref[t] = acc


def _bias_tiles(rel_bias, seq):
    blk = MOBA_BLOCK
    kpos = jnp.arange(blk, dtype=jnp.int32)[:, None]
    qpos = jnp.arange(blk, dtype=jnp.int32)[None, :]
    bucket = jnp.stack([_t5_bucket(qpos + blk - kpos), _t5_bucket(qpos - kpos)])
    kern = functools.partial(_bias_kernel, far_bucket=_far_bucket(seq))
    return pl.pallas_call(
        kern,
        grid=(MOBA_HEADS,),
        in_specs=[
            pl.BlockSpec((2, blk, blk), lambda h: (0, 0, 0)),
            pl.BlockSpec(memory_space=pltpu.SMEM),
        ],
        out_specs=pl.BlockSpec((None, 2, blk, blk), lambda h: (h, 0, 0, 0)),
        out_shape=jax.ShapeDtypeStruct((MOBA_HEADS, 2, blk, blk), F32),
        compiler_params=_params("arbitrary"),
        name="rel_bias_tiles",
    )(bucket, rel_bias)


PEN_ROWS = 16
DENOM_ROWS = 16


def _moba_kernel(qt_ref, k_ref, vt_ref, bias_ref, o_ref, kaug_ref, kmean_ref, qaug_ref, acc_ref, s_ref, *, nb):
    i = pl.program_id(1)
    blk, dh, heads = MOBA_BLOCK, MOBA_DH, MOBA_HEADS
    seq = nb * blk
    pair_w = 2 * dh
    assert pair_w == LANES and nb <= PEN_ROWS <= dh
    blk_shift = blk.bit_length() - 1
    assert (1 << blk_shift) == blk
    lane = lax.broadcasted_iota(jnp.int32, (1, pair_w), 1)

    @pl.when(i == 0)
    def _build_keys():
        rowblk = lax.broadcasted_iota(jnp.int32, (seq, pair_w), 0) >> blk_shift
        lane_s = lax.broadcasted_iota(jnp.int32, (seq, pair_w), 1)
        for p in range(heads // 2):
            kp = k_ref[:, p * pair_w:(p + 1) * pair_w]
            km = jnp.sum(kp.astype(F32).reshape(nb, blk, pair_w), axis=1) * (1.0 / blk)
            kmean_ref[p] = km
            for e in range(2):
                own = (lane_s >= dh) if e else (lane_s < dh)
                onehot = (lane_s - (0 if e else dh)) == rowblk
                kaug_ref[2 * p + e] = jnp.where(own, kp, onehot.astype(BF16))

    row8 = lax.broadcasted_iota(jnp.int32, (nb, blk), 0)
    scale = jnp.asarray(dh ** -0.5, BF16)

    gates = []
    for p in range(heads // 2):
        qtp = qt_ref[p * pair_w:(p + 1) * pair_w, :]
        km = kmean_ref[p]
        for e in range(2):
            own = (lane >= dh) if e else (lane < dh)
            km_hi, km_lo = _split_bf16(jnp.where(own, km, 0.0), 2)
            gates.append(_dot(km_hi, qtp) + _dot(km_lo, qtp))
    for p in range(heads // 2):
        qtp = qt_ref[p * pair_w:(p + 1) * pair_w, :]
        for e in range(2):
            g = jnp.where(row8 < i, gates[2 * p + e], -jnp.inf)
            rank = jnp.zeros((nb, blk), jnp.int32)
            for m in range(nb):
                gm = g[m:m + 1, :]
                beats = (gm > g) | ((gm == g) & (m < row8))
                rank = rank + beats.astype(jnp.int32)
            keep = ((rank < MOBA_TOPK) & (row8 < i)) | (row8 == i)
            pen_t = jnp.where(keep, 0.0, NEG_INF)
            pen = jnp.concatenate([pen_t, jnp.zeros((PEN_ROWS - nb, blk), F32)], axis=0).astype(BF16)
            qs = qtp[e * dh:(e + 1) * dh, :] * scale
            if e == 0:
                parts = [qs, pen, jnp.zeros((pair_w - dh - PEN_ROWS, blk), BF16)]
            else:
                parts = [pen, jnp.zeros((dh - PEN_ROWS, blk), BF16), qs]
            qaug_ref[2 * p + e] = jnp.concatenate(parts, axis=0)

    def scores(j, nblocks, slot):
        col_max = []
        for h in range(heads):
            s = _dot(kaug_ref[h, j * blk:(j + 1) * blk, :], qaug_ref[h])
            if j >= nblocks - 2:
                s = s + bias_ref[h, j - (nblocks - 2)]
            s_ref[slot, h] = s
            col_max.append(jnp.max(s, axis=0, keepdims=True))
        return col_max

    ones_rows = jnp.ones((DENOM_ROWS, blk), BF16)

    def attend(nblocks):
        col_max = scores(0, nblocks, 0)
        ms = [None] * heads
        for j in range(nblocks):
            next_max = scores(j + 1, nblocks, (j + 1) % 2) if j + 1 < nblocks else None
            for h in range(heads):
                m_new = col_max[h] if j == 0 else jnp.maximum(ms[h], col_max[h])
                pr = jnp.exp(s_ref[j % 2, h] - m_new).astype(BF16)
                vt_ones = jnp.concatenate([vt_ref[j, h * dh:(h + 1) * dh, :], ones_rows], axis=0)
                pv = _dot(vt_ones, pr)
                if j == 0:
                    acc_ref[h] = pv
                else:
                    acc_ref[h] = jnp.exp(ms[h] - m_new) * acc_ref[h] + pv
                ms[h] = m_new
            col_max = next_max
        for h in range(heads):
            o_ref[h * dh:(h + 1) * dh, :] = (acc_ref[h, 0:dh, :] / acc_ref[h, dh:dh + 1, :]).astype(BF16)

    for nblocks in range(1, nb + 1):
        pl.when(i == nblocks - 1)(functools.partial(attend, nblocks))


def _moba(qt, pb3, vt, bias, *, bcols):
    bsz, seq, _ = pb3.shape
    blk = MOBA_BLOCK
    nb = seq // blk
    assert seq % blk == 0
    kb0 = bcols["kb"][0]
    kern = functools.partial(_moba_kernel, nb=nb)
    return pl.pallas_call(
        kern,
        grid=(bsz, nb),
        in_specs=[
            pl.BlockSpec((None, None, MOBA_W, blk), lambda b, i: (b, i, 0, 0)),
            pl.BlockSpec((None, seq, MOBA_W), lambda b, i: (b, 0, kb0 // MOBA_W)),
            pl.BlockSpec((None, nb, MOBA_W, blk), lambda b, i: (b, 0, 0, 0)),
            _resident((MOBA_HEADS, 2, blk, blk), lambda b, i: (0, 0, 0, 0)),
        ],
        out_specs=pl.BlockSpec((None, None, MOBA_W, blk), lambda b, i: (b, i, 0, 0)),
        out_shape=jax.ShapeDtypeStruct((bsz, nb, MOBA_W, blk), BF16),
        scratch_shapes=[
            pltpu.VMEM((MOBA_HEADS, seq, 2 * MOBA_DH), BF16),
            pltpu.VMEM((MOBA_HEADS // 2, nb, 2 * MOBA_DH), F32),
            pltpu.VMEM((MOBA_HEADS, 2 * MOBA_DH, blk), BF16),
            pltpu.VMEM((MOBA_HEADS, MOBA_DH + DENOM_ROWS, blk), F32),
            pltpu.VMEM((2, MOBA_HEADS, blk, blk), F32),
        ],
        compiler_params=_params("arbitrary", "arbitrary"),
        name="moba",
    )(qt, pb3, vt, bias)


def _merge_kernel(oa_ref, obt_ref, g_ref, x_ref, wg_ref, wm_ref, wo_ref, o_ref, *, d):
    ya = _dot(oa_ref[...], wg_ref[...])
    wm = wm_ref[...]
    yb = jnp.concatenate([_dot_tn(obt_ref[s], wm) for s in range(obt_ref.shape[0])], axis=0)
    mixed = _sigmoid(g_ref[:, 0:d]) * ya + _sigmoid(g_ref[:, d:2 * d]) * yb
    o_ref[...] = x_ref[...] + _dot(mixed.astype(BF16), wo_ref[...])


def _merge(oa2, obt, pf2, x2, wg, wm, wo, layer, *, seq, tm):
    n, d = x2.shape
    blk = MOBA_BLOCK
    tpb = seq // tm
    kern = functools.partial(_merge_kernel, d=d)
    return pl.pallas_call(
        kern,
        grid=(n // tm,),
        in_specs=[
            pl.BlockSpec((tm, GLA_V_W), lambda i: (i, 0)),
            pl.BlockSpec((None, tm // blk, MOBA_W, blk), lambda i: (i // tpb, i % tpb, 0, 0)),
            pl.BlockSpec((tm, D_MERGE * d), lambda i: (i, 0)),
            pl.BlockSpec((tm, d), lambda i: (i, 0)),
            _resident((None, GLA_V_W, d), lambda i: (layer, 0, 0)),
            _resident((None, MOBA_W, d), lambda i: (layer, 0, 0)),
            _resident((None, d, d), lambda i: (layer, 0, 0)),
        ],
        out_specs=pl.BlockSpec((tm, d), lambda i: (i, 0)),
        out_shape=jax.ShapeDtypeStruct((n, d), F32),
        compiler_params=_params("parallel"),
        name="merge_out",
    )(oa2, obt, pf2, x2, wg, wm, wo)


HALO = 8


def _ffn_kernel(x_ref, g_ref, wu_ref, cw_ref, cb_ref, wd_ref, gf_ref, o_ref, carry_ref, act_ref,
                *, d_ff, cw, kgroup, final):
    t = pl.program_id(1)
    tm = x_ref.shape[0]

    @pl.when(t == 0)
    def _zero_history():
        carry_ref[...] = jnp.zeros_like(carry_ref)

    x = x_ref[...]
    ms = jnp.mean(x * x, axis=-1, keepdims=True)
    h = (x * lax.rsqrt(ms + EPS) * g_ref[...]).astype(BF16)

    row_h = lax.broadcasted_iota(jnp.int32, (HALO, cw), 0)

    def shifted(u, prev, s):
        rolled = pltpu.roll(u, s, axis=0)
        head = jnp.where(row_h < s, pltpu.roll(prev, s, axis=0), rolled[0:HALO, :])
        return jnp.concatenate([head, rolled[HALO:, :]], axis=0)

    y = x
    nchunks = d_ff // cw
    for c in range(nchunks):
        halves = []
        for part in range(2):
            c0 = part * d_ff + c * cw
            u = _dot(h, wu_ref[:, c0:c0 + cw])
            prev = carry_ref[:, c0:c0 + cw]
            carry_ref[:, c0:c0 + cw] = u[tm - HALO:tm, :]
            w = cw_ref[:, c0:c0 + cw]
            conv = cb_ref[:, c0:c0 + cw] + w[CONV_W - 1:CONV_W, :] * u
            for s in range(1, CONV_W):
                conv = conv + w[CONV_W - 1 - s:CONV_W - s, :] * shifted(u, prev, s)
            halves.append(conv)
        a, bval = halves
        act_ref[:, c * cw:(c + 1) * cw] = ((a * _sigmoid(a)) * bval).astype(BF16)
        if (c + 1) % kgroup == 0 or c == nchunks - 1:
            k0 = (c // kgroup) * kgroup * cw
            k1 = (c + 1) * cw
            y = y + _dot(act_ref[:, k0:k1], wd_ref[k0:k1, :])
    if final:
        ms2 = jnp.mean(y * y, axis=-1, keepdims=True)
        y = y * lax.rsqrt(ms2 + EPS) * gf_ref[...]
    o_ref[...] = y


def _ffn(x3, g, wu, cw_, cb, wd, gfinal, layer, *, tm, final):
    bsz, seq, d = x3.shape
    d_ff = wd.shape[1]
    cw = 256
    assert d_ff % cw == 0 and seq % tm == 0
    kern = functools.partial(_ffn_kernel, d_ff=d_ff, cw=cw, kgroup=4, final=final)
    return pl.pallas_call(
        kern,
        grid=(bsz, seq // tm),
        in_specs=[
            pl.BlockSpec((None, tm, d), lambda b, t: (b, t, 0)),
            pl.BlockSpec((None, 1, d), lambda b, t: (layer, 0, 0)),
            _resident((None, d, 2 * d_ff), lambda b, t: (layer, 0, 0)),
            pl.BlockSpec((None, CONV_W, 2 * d_ff), lambda b, t: (layer, 0, 0)),
            pl.BlockSpec((None, 1, 2 * d_ff), lambda b, t: (layer, 0, 0)),
            _resident((None, d_ff, d), lambda b, t: (layer, 0, 0)),
            pl.BlockSpec((1, d), lambda b, t: (0, 0)),
        ],
        out_specs=pl.BlockSpec((None, tm, d), lambda b, t: (b, t, 0)),
        out_shape=jax.ShapeDtypeStruct((bsz, seq, d), F32),
        scratch_shapes=[
            pltpu.VMEM((HALO, 2 * d_ff), F32),
            pltpu.VMEM((tm, d_ff), BF16),
        ],
        compiler_params=_params("arbitrary", "arbitrary"),
        name="ffn",
    )(x3, g, wu, cw_, cb, wd, gfinal)


def kernel(x, rel_bias, norm_mix, w_in, w_lr_up, b_forget, gla_out_norm, w_branch_gla, w_branch_moba,
           w_out, norm_ffn, w_up, conv_w, conv_b, w_down, norm_final):
    bsz, seq, d = x.shape
    depth = w_in.shape[0]
    n = bsz * seq
    fcols, wf, bcols, wb = _proj_layout(d)

    w_in_rows, w_in_t = _split_w_in(w_in, d, fcols, bcols)
    wlr = jnp.pad(w_lr_up, ((0, 0), (0, ALR_PAD - GLA_RANK), (0, 0)))
    wg = w_branch_gla.astype(BF16)
    wm = w_branch_moba.astype(BF16)
    wo = w_out.astype(BF16)
    wu = w_up.astype(BF16)
    wd = w_down.astype(BF16)
    norm_mix3 = norm_mix[:, None, :]
    norm_ffn3 = norm_ffn[:, None, :]
    b_forget3 = b_forget[:, None, :]
    gnorm3 = gla_out_norm[:, None, :]
    conv_b3 = conv_b[:, None, :]
    gfinal = norm_final[None, :]

    bias = _bias_tiles(rel_bias, seq)

    for l in range(depth):
        pf, pb, qt, vt = _in_proj(x.reshape(n, d), norm_mix3, w_in_rows, w_in_t, l,
                                  seq=seq, n_f32=len(fcols), wf=wf, wb=wb, tm=512)
        pf3 = pf.reshape(bsz, seq, wf)
        pb3 = pb.reshape(bsz, seq, wb)
        oa = _gla(pf3, pb3, wlr, b_forget3, gnorm3, l, fcols=fcols, bcols=bcols)
        obt = _moba(qt, pb3, vt, bias, bcols=bcols)
        x1 = _merge(oa.reshape(n, GLA_V_W), obt, pf, x.reshape(n, d), wg, wm, wo, l, seq=seq, tm=512)
        x = _ffn(x1.reshape(bsz, seq, d), norm_ffn3, wu, conv_w, conv_b3, wd, gfinal, l,
                 tm=512, final=(l == depth - 1))
    return x
```

```python
import functools
import math

import jax
import jax.numpy as jnp
import numpy as np
from jax import lax
from jax.experimental import pallas as pl
from jax.experimental.pallas import tpu as pltpu

GLA_HEADS = 4
GLA_DK = 128
GLA_DV = 256
GLA_RANK = 16
GLA_TAU = 16.0
GLA_CHUNK = 64
MOBA_HEADS = 8
MOBA_DH = 64
MOBA_BLOCK = 256
MOBA_TOPK = 3
REL_BUCKETS = 32
REL_MAX_DIST = 128
CONV_W = 3
EPS = 1e-6
NEG_INF = -1e30

GLA_QK_W = GLA_HEADS * GLA_DK
GLA_V_W = GLA_HEADS * GLA_DV
MOBA_W = MOBA_HEADS * MOBA_DH

LANES = 128
VMEM_LIMIT_BYTES = 56 * 1024 * 1024

F32 = jnp.float32
BF16 = jnp.bfloat16
HIGHEST = lax.Precision.HIGHEST

_NT = (((1,), (1,)), ((), ()))
_TN = (((0,), (0,)), ((), ()))


def _dot(a, b, precision=None):
    return jnp.dot(a, b, preferred_element_type=F32, precision=precision)


def _dot_nt(a, b, precision=None):
    return lax.dot_general(a, b, _NT, preferred_element_type=F32, precision=precision)


def _dot_tn(a, b):
    return lax.dot_general(a, b, _TN, preferred_element_type=F32)


def _split_bf16(x, n):
    parts = []
    for _ in range(n - 1):
        p = x.astype(BF16)
        parts.append(p)
        x = x - p.astype(F32)
    parts.append(x.astype(BF16))
    return parts


def _sigmoid(x):
    return 1.0 / (1.0 + jnp.exp(-x))


def _params(*semantics):
    return pltpu.CompilerParams(dimension_semantics=semantics, vmem_limit_bytes=VMEM_LIMIT_BYTES)


def _resident(shape, index_map):
    return pl.BlockSpec(shape, index_map, pipeline_mode=pl.Buffered(1))


D_MERGE = 2
ALR_PAD = LANES


def _proj_layout(d_model):
    f32_cols = dict(gates=(0, D_MERGE * d_model))
    off = D_MERGE * d_model
    for name, w in (("ra", GLA_V_W), ("qa", GLA_QK_W), ("ka", GLA_QK_W), ("alr", ALR_PAD)):
        f32_cols[name] = (off, w)
        off += w
    wf = off
    bf_cols = {}
    off = 0
    for name, w in (("va", GLA_V_W), ("kb", MOBA_W)):
        bf_cols[name] = (off, w)
        off += w
    return f32_cols, wf, bf_cols, off


def _split_w_in(w_in, d_model, f32_names, bf16_names):
    splits = (GLA_QK_W, GLA_QK_W, GLA_V_W, GLA_V_W, GLA_RANK, MOBA_W, MOBA_W, MOBA_W, D_MERGE * d_model)
    pts = np.cumsum(splits)[:-1].tolist()
    pieces = dict(zip(("qa", "ka", "va", "ra", "alr", "qb", "kb", "vb", "gates"), jnp.split(w_in, pts, axis=-1)))
    pieces["alr"] = jnp.pad(pieces["alr"], ((0, 0), (0, 0), (0, ALR_PAD - GLA_RANK)))
    rows = [pieces[k].astype(BF16) for k in tuple(f32_names) + tuple(bf16_names)]
    transposed = [jnp.swapaxes(pieces[k], 1, 2).astype(BF16) for k in ("qb", "vb")]
    return rows, transposed


def _in_proj_kernel(x_ref, g_ref, *refs, n_f32, n_bf16, chunk):
    w_f32 = refs[:n_f32]
    w_bf16 = refs[n_f32:n_f32 + n_bf16]
    wqt_ref, wvt_ref, of_ref, ob_ref, qt_ref, vt_ref = refs[n_f32 + n_bf16:]
    x = x_ref[...]
    ms = jnp.mean(x * x, axis=-1, keepdims=True)
    h = (x * lax.rsqrt(ms + EPS) * g_ref[...]).astype(BF16)
    for w_refs, out_ref in ((w_f32, of_ref), (w_bf16, ob_ref)):
        off = 0
        for w_ref in w_refs:
            width = w_ref.shape[1]
            for c0 in range(0, width, chunk):
                c1 = min(c0 + chunk, width)
                out_ref[:, off + c0:off + c1] = _dot(h, w_ref[:, c0:c1]).astype(out_ref.dtype)
            off += width
    blk = MOBA_BLOCK
    for s in range(x.shape[0] // blk):
        hs = h[s * blk:(s + 1) * blk, :]
        qt_ref[s] = _dot_nt(wqt_ref[...], hs).astype(BF16)
        vt_ref[s] = _dot_nt(wvt_ref[...], hs).astype(BF16)


def _in_proj(x2, g, w_rows, w_t, layer, *, seq, n_f32, wf, wb, tm):
    n, d = x2.shape
    blk = MOBA_BLOCK
    tpb = seq // tm
    spt = tm // blk
    kern = functools.partial(_in_proj_kernel, n_f32=n_f32, n_bf16=len(w_rows) - n_f32, chunk=512)
    t_shape = jax.ShapeDtypeStruct((n // seq, seq // blk, MOBA_W, blk), BF16)
    t_spec = pl.BlockSpec((None, spt, MOBA_W, blk), lambda i: (i // tpb, i % tpb, 0, 0))
    w_specs = [_resident((None,) + w.shape[1:], lambda i: (layer, 0, 0)) for w in tuple(w_rows) + tuple(w_t)]
    return pl.pallas_call(
        kern,
        grid=(n // tm,),
        in_specs=[
            pl.BlockSpec((tm, d), lambda i: (i, 0)),
            pl.BlockSpec((None, 1, d), lambda i: (layer, 0, 0)),
            *w_specs,
        ],
        out_specs=[
            pl.BlockSpec((tm, wf), lambda i: (i, 0)),
            pl.BlockSpec((tm, wb), lambda i: (i, 0)),
            t_spec,
            t_spec,
        ],
        out_shape=[jax.ShapeDtypeStruct((n, wf), F32), jax.ShapeDtypeStruct((n, wb), BF16), t_shape, t_shape],
        compiler_params=_params("parallel"),
        name="in_proj",
    )(x2, g, *w_rows, *w_t)


GLA_SUPER = 256
GLA_PARTS = 4


def _gla_kernel(q_ref, k_ref, v_ref, r_ref, alr_ref, wlr_ref, bf_ref, gn_ref, o_ref,
                qd_ref, u_ref, dec_ref, oi_ref, sb_ref, b_ref, kd_ref, ke_ref, at_ref, la_ref, *, seq):
    c, sb = GLA_CHUNK, GLA_SUPER
    cps = sb // c
    nsb = seq // sb
    nchunks = seq // c
    shift = c.bit_length() - 1
    assert (1 << shift) == c and seq % sb == 0
    row = lax.broadcasted_iota(jnp.int32, (sb, sb), 0)
    col = lax.broadcasted_iota(jnp.int32, (sb, sb), 1)
    same_chunk = (row >> shift) == (col >> shift)
    causal = same_chunk & (row >= col)
    tril = causal.astype(BF16)
    wlr_parts = _split_bf16(wlr_ref[...], 2)
    bfg = bf_ref[...]
    gn = gn_ref[...]
    q_scale = GLA_DK ** -0.5

    blocks = [slice(s * sb, (s + 1) * sb) for s in range(nsb)]

    for rows in blocks:
        a_hi, a_lo = _split_bf16(alr_ref[rows, :], 2)
        xa = _dot(a_hi, wlr_parts[0]) + _dot(a_hi, wlr_parts[1]) + _dot(a_lo, wlr_parts[0]) + bfg
        log_a = (jnp.minimum(xa, 0.0) - jnp.log1p(jnp.exp(-jnp.abs(xa)))) * (1.0 / GLA_TAU)
        la_ref[rows, :] = jnp.concatenate(_split_bf16(log_a, 3), axis=1)

    for rows in blocks:
        pieces = _dot(tril, la_ref[rows, :])
        b_ref[rows, :] = (pieces[:, 0:GLA_DK] + pieces[:, GLA_DK:2 * GLA_DK]) + pieces[:, 2 * GLA_DK:3 * GLA_DK]

    for s, rows in enumerate(blocks):
        b = b_ref[rows, :]
        b_last = jnp.concatenate(
            [jnp.broadcast_to(b[(cc + 1) * c - 1:(cc + 1) * c, :], (c, GLA_DK)) for cc in range(cps)], axis=0)
        k = k_ref[rows, :]
        qd_ref[rows, :] = ((q_ref[rows, :] * q_scale) * jnp.exp(b)).astype(BF16)
        kd_ref[rows, :] = (k * jnp.exp(-b)).astype(BF16)
        ke_ref[rows, :] = (k * jnp.exp(b_last - b)).astype(BF16)
        chunk_decay = jnp.exp(b_last)
        for cc in range(cps):
            dec_ref[s * cps + cc] = chunk_decay[cc * c:cc * c + 8, :]

    def intra(part):
        for rows in part:
            at_ref[rows, :] = jnp.where(causal, _dot_nt(qd_ref[rows, :], kd_ref[rows, :]), 0.0).astype(BF16)
        for rows in part:
            for n in range(rows.start // c, rows.stop // c):
                u_ref[n] = _dot_tn(v_ref[n * c:(n + 1) * c, :], ke_ref[n * c:(n + 1) * c, :])
        for rows in part:
            oi_ref[rows, :] = _dot(at_ref[rows, :], v_ref[rows, :])

    def scan(part, st):
        for n in range(part[0].start // c, part[-1].stop // c):
            sb_ref[n] = st.astype(BF16)
            st = st * dec_ref[n, 0:1, :] + u_ref[n]
        return st

    def finish(part):
        for rows in part:
            for n in range(rows.start // c, rows.stop // c):
                crow = slice(n * c, (n + 1) * c)
                oi_ref[crow, :] = oi_ref[crow, :] + _dot_nt(qd_ref[crow, :], sb_ref[n])
        for rows in part:
            o = oi_ref[rows, :]
            ms = jnp.mean(o * o, axis=-1, keepdims=True)
            r = r_ref[rows, :]
            out = (o * lax.rsqrt(ms + EPS) * gn) * (r * _sigmoid(r))
            o_ref[rows, :] = out.astype(BF16)

    per_part = nsb // GLA_PARTS
    parts = [blocks[p * per_part:(p + 1) * per_part] for p in range(GLA_PARTS)]
    state = jnp.zeros((GLA_DV, GLA_DK), F32)
    intra(parts[0])
    for p in range(GLA_PARTS):
        state = scan(parts[p], state)
        if p + 1 < GLA_PARTS:
            intra(parts[p + 1])
        finish(parts[p])


def _gla(pf3, pb3, wlr, bforget, gnorm, layer, *, fcols, bcols):
    bsz, seq, _ = pf3.shape
    qa0, ka0, ra0, alr0 = (fcols[k][0] for k in ("qa", "ka", "ra", "alr"))
    va0 = bcols["va"][0]
    kern = functools.partial(_gla_kernel, seq=seq)
    return pl.pallas_call(
        kern,
        grid=(bsz, GLA_HEADS),
        in_specs=[
            pl.BlockSpec((None, seq, GLA_DK), lambda b, h: (b, 0, qa0 // GLA_DK + h)),
            pl.BlockSpec((None, seq, GLA_DK), lambda b, h: (b, 0, ka0 // GLA_DK + h)),
            pl.BlockSpec((None, seq, GLA_DV), lambda b, h: (b, 0, va0 // GLA_DV + h)),
            pl.BlockSpec((None, seq, GLA_DV), lambda b, h: (b, 0, ra0 // GLA_DV + h)),
            pl.BlockSpec((None, seq, ALR_PAD), lambda b, h: (b, 0, alr0 // ALR_PAD)),
            pl.BlockSpec((None, ALR_PAD, GLA_DK), lambda b, h: (layer, 0, h)),
            pl.BlockSpec((None, 1, GLA_DK), lambda b, h: (layer, 0, h)),
            pl.BlockSpec((None, 1, GLA_DV), lambda b, h: (layer, 0, h)),
        ],
        out_specs=pl.BlockSpec((None, seq, GLA_DV), lambda b, h: (b, 0, h)),
        out_shape=jax.ShapeDtypeStruct((bsz, seq, GLA_V_W), BF16),
        scratch_shapes=[
            pltpu.VMEM((seq, GLA_DK), BF16),
            pltpu.VMEM((seq // GLA_CHUNK, GLA_DV, GLA_DK), F32),
            pltpu.VMEM((seq // GLA_CHUNK, 8, GLA_DK), F32),
            pltpu.VMEM((seq, GLA_DV), F32),
            pltpu.VMEM((seq // GLA_CHUNK, GLA_DV, GLA_DK), BF16),
            pltpu.VMEM((seq, GLA_DK), F32),
            pltpu.VMEM((seq, GLA_DK), BF16),
            pltpu.VMEM((seq, GLA_DK), BF16),
            pltpu.VMEM((seq, GLA_SUPER), BF16),
            pltpu.VMEM((seq, 3 * GLA_DK), BF16),
        ],
        compiler_params=_params("parallel", "parallel"),
        name="gla",
    )(pf3, pf3, pb3, pf3, pf3, wlr, bforget, gnorm)


def _t5_bucket(rel):
    n = jnp.maximum(rel, 0)
    max_exact = REL_BUCKETS // 2
    nf = jnp.maximum(n, 1).astype(F32)
    large = max_exact + (jnp.log(nf / max_exact) / math.log(REL_MAX_DIST / max_exact)
                         * (REL_BUCKETS - max_exact)).astype(jnp.int32)
    large = jnp.minimum(large, REL_BUCKETS - 1)
    return jnp.where(n < max_exact, n, large)


def _far_bucket(seq):
    d = np.arange(MOBA_BLOCK + 1, max(seq, MOBA_BLOCK + 2), dtype=np.float64)
    max_exact = REL_BUCKETS // 2
    large = max_exact + np.floor(np.log(d / max_exact) / math.log(REL_MAX_DIST / max_exact)
                                 * (REL_BUCKETS - max_exact) + 1e-6)
    assert MOBA_BLOCK + 1 >= max_exact and np.all(large >= REL_BUCKETS - 1), "far blocks must share one bucket"
    return REL_BUCKETS - 1


def _bias_kernel(bucket_ref, rb_ref, o_ref, *, far_bucket):
    h = pl.program_id(0)
    blk = MOBA_BLOCK
    far = rb_ref[far_bucket, h]
    row = lax.broadcasted_iota(jnp.int32, (blk, blk), 0)
    col = lax.broadcasted_iota(jnp.int32, (blk, blk), 1)
    for t in range(2):
        bk = bucket_ref[t]
        acc = jnp.zeros((blk, blk), F32)
        for b in range(REL_BUCKETS):
            acc = jnp.where(bk == b, rb_ref[b, h], acc)
        acc = acc - far
        if t == 1:
            acc = jnp.where(col >= row, acc, NEG_INF)
        o_ref[t] = acc


def _bias_tiles(rel_bias, seq):
    blk = MOBA_BLOCK
    kpos = jnp.arange(blk, dtype=jnp.int32)[:, None]
    qpos = jnp.arange(blk, dtype=jnp.int32)[None, :]
    bucket = jnp.stack([_t5_bucket(qpos + blk - kpos), _t5_bucket(qpos - kpos)])
    kern = functools.partial(_bias_kernel, far_bucket=_far_bucket(seq))
    return pl.pallas_call(
        kern,
        grid=(MOBA_HEADS,),
        in_specs=[
            pl.BlockSpec((2, blk, blk), lambda h: (0, 0, 0)),
            pl.BlockSpec(memory_space=pltpu.SMEM),
        ],
        out_specs=pl.BlockSpec((None, 2, blk, blk), lambda h: (h, 0, 0, 0)),
        out_shape=jax.ShapeDtypeStruct((MOBA_HEADS, 2, blk, blk), F32),
        compiler_params=_params("arbitrary"),
        name="rel_bias_tiles",
    )(bucket, rel_bias)


PEN_ROWS = 16
DENOM_ROWS = 16


def _moba_kernel(qt_ref, k_ref, vt_ref, bias_ref, o_ref, kaug_ref, kmean_ref, qaug_ref, acc_ref, s_ref, *, nb):
    i = pl.program_id(1)
    blk, dh, heads = MOBA_BLOCK, MOBA_DH, MOBA_HEADS
    seq = nb * blk
    pair_w = 2 * dh
    assert pair_w == LANES and nb <= PEN_ROWS <= dh
    blk_shift = blk.bit_length() - 1
    assert (1 << blk_shift) == blk
    lane = lax.broadcasted_iota(jnp.int32, (1, pair_w), 1)

    @pl.when(i == 0)
    def _build_keys():
        rowblk = lax.broadcasted_iota(jnp.int32, (seq, pair_w), 0) >> blk_shift
        lane_s = lax.broadcasted_iota(jnp.int32, (seq, pair_w), 1)
        for p in range(heads // 2):
            kp = k_ref[:, p * pair_w:(p + 1) * pair_w]
            km = jnp.sum(kp.astype(F32).reshape(nb, blk, pair_w), axis=1) * (1.0 / blk)
            kmean_ref[p] = km
            for e in range(2):
                own = (lane_s >= dh) if e else (lane_s < dh)
                onehot = (lane_s - (0 if e else dh)) == rowblk
                kaug_ref[2 * p + e] = jnp.where(own, kp, onehot.astype(BF16))

    row8 = lax.broadcasted_iota(jnp.int32, (nb, blk), 0)
    scale = jnp.asarray(dh ** -0.5, BF16)

    gates = []
    for p in range(heads // 2):
        qtp = qt_ref[p * pair_w:(p + 1) * pair_w, :]
        km = kmean_ref[p]
        for e in range(2):
            own = (lane >= dh) if e else (lane < dh)
            km_hi, km_lo = _split_bf16(jnp.where(own, km, 0.0), 2)
            gates.append(_dot(km_hi, qtp) + _dot(km_lo, qtp))
    for p in range(heads // 2):
        qtp = qt_ref[p * pair_w:(p + 1) * pair_w, :]
        for e in range(2):
            g = jnp.where(row8 < i, gates[2 * p + e], -jnp.inf)
            rank = jnp.zeros((nb, blk), jnp.int32)
            for m in range(nb):
                gm = g[m:m + 1, :]
                beats = (gm > g) | ((gm == g) & (m < row8))
                rank = rank + beats.astype(jnp.int32)
            keep = ((rank < MOBA_TOPK) & (row8 < i)) | (row8 == i)
            pen_t = jnp.where(keep, 0.0, NEG_INF)
            pen = jnp.concatenate([pen_t, jnp.zeros((PEN_ROWS - nb, blk), F32)], axis=0).astype(BF16)
            qs = qtp[e * dh:(e + 1) * dh, :] * scale
            if e == 0:
                parts = [qs, pen, jnp.zeros((pair_w - dh - PEN_ROWS, blk), BF16)]
            else:
                parts = [pen, jnp.zeros((dh - PEN_ROWS, blk), BF16), qs]
            qaug_ref[2 * p + e] = jnp.concatenate(parts, axis=0)

    def scores(j, nblocks, slot):
        col_max = []
        for h in range(heads):
            s = _dot(kaug_ref[h, j * blk:(j + 1) * blk, :], qaug_ref[h])
            if j >= nblocks - 2:
                s = s + bias_ref[h, j - (nblocks - 2)]
            s_ref[slot, h] = s
            col_max.append(jnp.max(s, axis=0, keepdims=True))
        return col_max

    ones_rows = jnp.ones((DENOM_ROWS, blk), BF16)

    def attend(nblocks):
        col_max = scores(0, nblocks, 0)
        ms = [None] * heads
        for j in range(nblocks):
            next_max = scores(j + 1, nblocks, (j + 1) % 2) if j + 1 < nblocks else None
            for h in range(heads):
                m_new = col_max[h] if j == 0 else jnp.maximum(ms[h], col_max[h])
                pr = jnp.exp(s_ref[j % 2, h] - m_new).astype(BF16)
                vt_ones = jnp.concatenate([vt_ref[j, h * dh:(h + 1) * dh, :], ones_rows], axis=0)
                pv = _dot(vt_ones, pr)
                if j == 0:
                    acc_ref[h] = pv
                else:
                    acc_ref[h] = jnp.exp(ms[h] - m_new) * acc_ref[h] + pv
                ms[h] = m_new
            col_max = next_max
        for h in range(heads):
            o_ref[h * dh:(h + 1) * dh, :] = (acc_ref[h, 0:dh, :] / acc_ref[h, dh:dh + 1, :]).astype(BF16)

    for nblocks in range(1, nb + 1):
        pl.when(i == nblocks - 1)(functools.partial(attend, nblocks))


def _moba(qt, pb3, vt, bias, *, bcols):
    bsz, seq, _ = pb3.shape
    blk = MOBA_BLOCK
    nb = seq // blk
    assert seq % blk == 0
    kb0 = bcols["kb"][0]
    kern = functools.partial(_moba_kernel, nb=nb)
    return pl.pallas_call(
        kern,
        grid=(bsz, nb),
        in_specs=[
            pl.BlockSpec((None, None, MOBA_W, blk), lambda b, i: (b, i, 0, 0)),
            pl.BlockSpec((None, seq, MOBA_W), lambda b, i: (b, 0, kb0 // MOBA_W)),
            pl.BlockSpec((None, nb, MOBA_W, blk), lambda b, i: (b, 0, 0, 0)),
            _resident((MOBA_HEADS, 2, blk, blk), lambda b, i: (0, 0, 0, 0)),
        ],
        out_specs=pl.BlockSpec((None, None, MOBA_W, blk), lambda b, i: (b, i, 0, 0)),
        out_shape=jax.ShapeDtypeStruct((bsz, nb, MOBA_W, blk), BF16),
        scratch_shapes=[
            pltpu.VMEM((MOBA_HEADS, seq, 2 * MOBA_DH), BF16),
            pltpu.VMEM((MOBA_HEADS // 2, nb, 2 * MOBA_DH), F32),
            pltpu.VMEM((MOBA_HEADS, 2 * MOBA_DH, blk), BF16),
            pltpu.VMEM((MOBA_HEADS, MOBA_DH + DENOM_ROWS, blk), F32),
            pltpu.VMEM((2, MOBA_HEADS, blk, blk), F32),
        ],
        compiler_params=_params("arbitrary", "arbitrary"),
        name="moba",
    )(qt, pb3, vt, bias)


HALO = 8


def _mix_ffn_kernel(oa_ref, obt_ref, gate_ref, x_ref, wg_ref, wm_ref, wo_ref, g_ref, wu_ref, cw_ref, cb_ref,
                    wd_ref, gf_ref, o_ref, carry_ref, act_ref, *, d_ff, cw, kgroup, final):
    t = pl.program_id(1)
    tm, d = x_ref.shape

    @pl.when(t == 0)
    def _zero_history():
        carry_ref[...] = jnp.zeros_like(carry_ref)

    ya = _dot(oa_ref[...], wg_ref[...])
    wm = wm_ref[...]
    yb = jnp.concatenate([_dot_tn(obt_ref[s], wm) for s in range(obt_ref.shape[0])], axis=0)
    mixed = _sigmoid(gate_ref[:, 0:d]) * ya + _sigmoid(gate_ref[:, d:2 * d]) * yb
    x = x_ref[...] + _dot(mixed.astype(BF16), wo_ref[...])

    ms = jnp.mean(x * x, axis=-1, keepdims=True)
    h = (x * lax.rsqrt(ms + EPS) * g_ref[...]).astype(BF16)

    row_h = lax.broadcasted_iota(jnp.int32, (HALO, cw), 0)

    def shifted(u, prev, s):
        rolled = pltpu.roll(u, s, axis=0)
        head = jnp.where(row_h < s, pltpu.roll(prev, s, axis=0), rolled[0:HALO, :])
        return jnp.concatenate([head, rolled[HALO:, :]], axis=0)

    y = x
    nchunks = d_ff // cw
    for c in range(nchunks):
        halves = []
        for part in range(2):
            c0 = part * d_ff + c * cw
            u = _dot(h, wu_ref[:, c0:c0 + cw])
            prev = carry_ref[:, c0:c0 + cw]
            carry_ref[:, c0:c0 + cw] = u[tm - HALO:tm, :]
            w = cw_ref[:, c0:c0 + cw]
            conv = cb_ref[:, c0:c0 + cw] + w[CONV_W - 1:CONV_W, :] * u
            for s in range(1, CONV_W):
                conv = conv + w[CONV_W - 1 - s:CONV_W - s, :] * shifted(u, prev, s)
            halves.append(conv)
        a, bval = halves
        act_ref[:, c * cw:(c + 1) * cw] = ((a * _sigmoid(a)) * bval).astype(BF16)
        if (c + 1) % kgroup == 0 or c == nchunks - 1:
            k0 = (c // kgroup) * kgroup * cw
            k1 = (c + 1) * cw
            y = y + _dot(act_ref[:, k0:k1], wd_ref[k0:k1, :])
    if final:
        ms2 = jnp.mean(y * y, axis=-1, keepdims=True)
        y = y * lax.rsqrt(ms2 + EPS) * gf_ref[...]
    o_ref[...] = y


def _mix_ffn(oa3, obt, pf3, x3, wg, wm, wo, g, wu, cw_, cb, wd, gfinal, layer, *, tm, final):
    bsz, seq, d = x3.shape
    d_ff = wd.shape[1]
    blk = MOBA_BLOCK
    cw = 256
    assert d_ff % cw == 0 and seq % tm == 0 and tm % blk == 0
    kern = functools.partial(_mix_ffn_kernel, d_ff=d_ff, cw=cw, kgroup=4, final=final)
    return pl.pallas_call(
        kern,
        grid=(bsz, seq // tm),
        in_specs=[
            pl.BlockSpec((None, tm, GLA_V_W), lambda b, t: (b, t, 0)),
            pl.BlockSpec((None, tm // blk, MOBA_W, blk), lambda b, t: (b, t, 0, 0)),
            pl.BlockSpec((None, tm, D_MERGE * d), lambda b, t: (b, t, 0)),
            pl.BlockSpec((None, tm, d), lambda b, t: (b, t, 0)),
            _resident((None, GLA_V_W, d), lambda b, t: (layer, 0, 0)),
            _resident((None, MOBA_W, d), lambda b, t: (layer, 0, 0)),
            _resident((None, d, d), lambda b, t: (layer, 0, 0)),
            pl.BlockSpec((None, 1, d), lambda b, t: (layer, 0, 0)),
            _resident((None, d, 2 * d_ff), lambda b, t: (layer, 0, 0)),
            pl.BlockSpec((None, CONV_W, 2 * d_ff), lambda b, t: (layer, 0, 0)),
            pl.BlockSpec((None, 1, 2 * d_ff), lambda b, t: (layer, 0, 0)),
            _resident((None, d_ff, d), lambda b, t: (layer, 0, 0)),
            pl.BlockSpec((1, d), lambda b, t: (0, 0)),
        ],
        out_specs=pl.BlockSpec((None, tm, d), lambda b, t: (b, t, 0)),
        out_shape=jax.ShapeDtypeStruct((bsz, seq, d), F32),
        scratch_shapes=[
            pltpu.VMEM((HALO, 2 * d_ff), F32),
            pltpu.VMEM((tm, d_ff), BF16),
        ],
        compiler_params=_params("arbitrary", "arbitrary"),
        name="mix_ffn",
    )(oa3, obt, pf3, x3, wg, wm, wo, g, wu, cw_, cb, wd, gfinal)


def kernel(x, rel_bias, norm_mix, w_in, w_lr_up, b_forget, gla_out_norm, w_branch_gla, w_branch_moba,
           w_out, norm_ffn, w_up, conv_w, conv_b, w_down, norm_final):
    bsz, seq, d = x.shape
    depth = w_in.shape[0]
    n = bsz * seq
    fcols, wf, bcols, wb = _proj_layout(d)

    w_in_rows, w_in_t = _split_w_in(w_in, d, fcols, bcols)
    wlr = jnp.pad(w_lr_up, ((0, 0), (0, ALR_PAD - GLA_RANK), (0, 0)))
    wg = w_branch_gla.astype(BF16)
    wm = w_branch_moba.astype(BF16)
    wo = w_out.astype(BF16)
    wu = w_up.astype(BF16)
    wd = w_down.astype(BF16)
    norm_mix3 = norm_mix[:, None, :]
    norm_ffn3 = norm_ffn[:, None, :]
    b_forget3 = b_forget[:, None, :]
    gnorm3 = gla_out_norm[:, None, :]
    conv_b3 = conv_b[:, None, :]
    gfinal = norm_final[None, :]

    bias = _bias_tiles(rel_bias, seq)

    for l in range(depth):
        pf, pb, qt, vt = _in_proj(x.reshape(n, d), norm_mix3, w_in_rows, w_in_t, l,
                                  seq=seq, n_f32=len(fcols), wf=wf, wb=wb, tm=512)
        pf3 = pf.reshape(bsz, seq, wf)
        pb3 = pb.reshape(bsz, seq, wb)
        oa = _gla(pf3, pb3, wlr, b_forget3, gnorm3, l, fcols=fcols, bcols=bcols)
        obt = _moba(qt, pb3, vt, bias, bcols=bcols)
        assert fcols["gates"][0] == 0
        x = _mix_ffn(oa, obt, pf3, x, wg, wm, wo, norm_ffn3, wu, conv_w, conv_b3, wd, gfinal, l,
                     tm=512, final=(l == depth - 1))
    return x
```

```python
import functools
import math

import jax
import jax.numpy as jnp
import numpy as np
from jax import lax
from jax.experimental import pallas as pl
from jax.experimental.pallas import tpu as pltpu

GLA_HEADS = 4
GLA_DK = 128
GLA_DV = 256
GLA_RANK = 16
GLA_TAU = 16.0
GLA_CHUNK = 64
MOBA_HEADS = 8
MOBA_DH = 64
MOBA_BLOCK = 256
MOBA_TOPK = 3
REL_BUCKETS = 32
REL_MAX_DIST = 128
CONV_W = 3
EPS = 1e-6
NEG_INF = -1e30

GLA_QK_W = GLA_HEADS * GLA_DK
GLA_V_W = GLA_HEADS * GLA_DV
MOBA_W = MOBA_HEADS * MOBA_DH

LANES = 128
VMEM_LIMIT_BYTES = 56 * 1024 * 1024

F32 = jnp.float32
BF16 = jnp.bfloat16
HIGHEST = lax.Precision.HIGHEST

_NT = (((1,), (1,)), ((), ()))
_TN = (((0,), (0,)), ((), ()))


def _dot(a, b, precision=None):
    return jnp.dot(a, b, preferred_element_type=F32, precision=precision)


def _dot_nt(a, b, precision=None):
    return lax.dot_general(a, b, _NT, preferred_element_type=F32, precision=precision)


def _dot_tn(a, b):
    return lax.dot_general(a, b, _TN, preferred_element_type=F32)


def _split_bf16(x, n):
    parts = []
    for _ in range(n - 1):
        p = x.astype(BF16)
        parts.append(p)
        x = x - p.astype(F32)
    parts.append(x.astype(BF16))
    return parts


def _sigmoid(x):
    return 1.0 / (1.0 + jnp.exp(-x))


def _params(*semantics):
    return pltpu.CompilerParams(dimension_semantics=semantics, vmem_limit_bytes=VMEM_LIMIT_BYTES)


def _resident(shape, index_map):
    return pl.BlockSpec(shape, index_map, pipeline_mode=pl.Buffered(1))


D_MERGE = 2
ALR_PAD = LANES


def _proj_layout(d_model):
    f32_cols = dict(gates=(0, D_MERGE * d_model))
    off = D_MERGE * d_model
    for name, w in (("ra", GLA_V_W), ("qa", GLA_QK_W), ("ka", GLA_QK_W), ("alr", ALR_PAD)):
        f32_cols[name] = (off, w)
        off += w
    wf = off
    bf_cols = {}
    off = 0
    for name, w in (("va", GLA_V_W), ("kb", MOBA_W)):
        bf_cols[name] = (off, w)
        off += w
    return f32_cols, wf, bf_cols, off


_W_IN_NAMES = ("qa", "ka", "va", "ra", "alr", "qb", "kb", "vb", "gates")
W_PREP_ROWS = 128


def _w_in_offsets(d_model):
    splits = (GLA_QK_W, GLA_QK_W, GLA_V_W, GLA_V_W, GLA_RANK, MOBA_W, MOBA_W, MOBA_W, D_MERGE * d_model)
    starts = np.concatenate([[0], np.cumsum(splits)[:-1]]).tolist()
    return {k: (int(s), int(w)) for k, s, w in zip(_W_IN_NAMES, starts, splits)}


def _split_w_in_kernel(w_ref, *out_refs, offsets, row_names):
    row_refs = out_refs[:len(row_names)]
    qbt_ref, vbt_ref = out_refs[len(row_names):]
    for name, ref in zip(row_names, row_refs):
        start, width = offsets[name]
        if name == "alr":
            lane = lax.broadcasted_iota(jnp.int32, (w_ref.shape[0], ALR_PAD), 1)
            tile = w_ref[:, start:start + ALR_PAD]
            ref[...] = jnp.where(lane < width, tile, 0.0).astype(BF16)
        else:
            ref[...] = w_ref[:, start:start + width].astype(BF16)
    for name, ref in (("qb", qbt_ref), ("vb", vbt_ref)):
        start, width = offsets[name]
        ref[...] = w_ref[:, start:start + width].T.astype(BF16)


def _split_w_in(w_in, d_model, f32_names, bf16_names):
    depth, d, d_in = w_in.shape
    offsets = _w_in_offsets(d_model)
    row_names = tuple(f32_names) + tuple(bf16_names)
    rt = W_PREP_ROWS
    widths = [ALR_PAD if k == "alr" else offsets[k][1] for k in row_names]
    kern = functools.partial(_split_w_in_kernel, offsets=offsets, row_names=row_names)
    outs = pl.pallas_call(
        kern,
        grid=(depth, d // rt),
        in_specs=[pl.BlockSpec((None, rt, d_in), lambda l, r: (l, r, 0))],
        out_specs=[pl.BlockSpec((None, rt, w), lambda l, r: (l, r, 0)) for w in widths]
        + [pl.BlockSpec((None, MOBA_W, rt), lambda l, r: (l, 0, r))] * 2,
        out_shape=[jax.ShapeDtypeStruct((depth, d, w), BF16) for w in widths]
        + [jax.ShapeDtypeStruct((depth, MOBA_W, d), BF16)] * 2,
        compiler_params=_params("parallel", "parallel"),
        name="split_w_in",
    )(w_in)
    return outs[:len(row_names)], outs[len(row_names):]


def _in_proj_kernel(x_ref, g_ref, *refs, n_f32, n_bf16, chunk):
    w_f32 = refs[:n_f32]
    w_bf16 = refs[n_f32:n_f32 + n_bf16]
    wqt_ref, wvt_ref, of_ref, ob_ref, qt_ref, vt_ref = refs[n_f32 + n_bf16:]
    x = x_ref[...]
    ms = jnp.mean(x * x, axis=-1, keepdims=True)
    h = (x * lax.rsqrt(ms + EPS) * g_ref[...]).astype(BF16)
    for w_refs, out_ref in ((w_f32, of_ref), (w_bf16, ob_ref)):
        off = 0
        for w_ref in w_refs:
            width = w_ref.shape[1]
            for c0 in range(0, width, chunk):
                c1 = min(c0 + chunk, width)
                out_ref[:, off + c0:off + c1] = _dot(h, w_ref[:, c0:c1]).astype(out_ref.dtype)
            off += width
    blk = MOBA_BLOCK
    for s in range(x.shape[0] // blk):
        hs = h[s * blk:(s + 1) * blk, :]
        qt_ref[s] = _dot_nt(wqt_ref[...], hs).astype(BF16)
        vt_ref[s] = _dot_nt(wvt_ref[...], hs).astype(BF16)


def _in_proj(x2, g, w_rows, w_t, layer, *, seq, n_f32, wf, wb, tm):
    n, d = x2.shape
    blk = MOBA_BLOCK
    tpb = seq // tm
    spt = tm // blk
    kern = functools.partial(_in_proj_kernel, n_f32=n_f32, n_bf16=len(w_rows) - n_f32, chunk=512)
    t_shape = jax.ShapeDtypeStruct((n // seq, seq // blk, MOBA_W, blk), BF16)
    t_spec = pl.BlockSpec((None, spt, MOBA_W, blk), lambda i: (i // tpb, i % tpb, 0, 0))
    w_specs = [_resident((None,) + w.shape[1:], lambda i: (layer, 0, 0)) for w in tuple(w_rows) + tuple(w_t)]
    return pl.pallas_call(
        kern,
        grid=(n // tm,),
        in_specs=[
            pl.BlockSpec((tm, d), lambda i: (i, 0)),
            pl.BlockSpec((None, 1, d), lambda i: (layer, 0, 0)),
            *w_specs,
        ],
        out_specs=[
            pl.BlockSpec((tm, wf), lambda i: (i, 0)),
            pl.BlockSpec((tm, wb), lambda i: (i, 0)),
            t_spec,
            t_spec,
        ],
        out_shape=[jax.ShapeDtypeStruct((n, wf), F32), jax.ShapeDtypeStruct((n, wb), BF16), t_shape, t_shape],
        compiler_params=_params("parallel"),
        name="in_proj",
    )(x2, g, *w_rows, *w_t)


GLA_SUPER = 256
GLA_PARTS = 4


def _gla_kernel(q_ref, k_ref, v_ref, r_ref, alr_ref, wlr_ref, bf_ref, gn_ref, o_ref,
                qd_ref, u_ref, dec_ref, oi_ref, sb_ref, b_ref, kd_ref, ke_ref, at_ref, la_ref, *, seq):
    c, sb = GLA_CHUNK, GLA_SUPER
    cps = sb // c
    nsb = seq // sb
    nchunks = seq // c
    shift = c.bit_length() - 1
    assert (1 << shift) == c and seq % sb == 0
    row = lax.broadcasted_iota(jnp.int32, (sb, sb), 0)
    col = lax.broadcasted_iota(jnp.int32, (sb, sb), 1)
    same_chunk = (row >> shift) == (col >> shift)
    causal = same_chunk & (row >= col)
    tril = causal.astype(BF16)
    wlr_parts = _split_bf16(wlr_ref[...], 2)
    bfg = bf_ref[...]
    gn = gn_ref[...]
    q_scale = GLA_DK ** -0.5

    blocks = [slice(s * sb, (s + 1) * sb) for s in range(nsb)]

    for rows in blocks:
        a_hi, a_lo = _split_bf16(alr_ref[rows, :], 2)
        xa = _dot(a_hi, wlr_parts[0]) + _dot(a_hi, wlr_parts[1]) + _dot(a_lo, wlr_parts[0]) + bfg
        log_a = (jnp.minimum(xa, 0.0) - jnp.log1p(jnp.exp(-jnp.abs(xa)))) * (1.0 / GLA_TAU)
        la_ref[rows, :] = jnp.concatenate(_split_bf16(log_a, 3), axis=1)

    for rows in blocks:
        pieces = _dot(tril, la_ref[rows, :])
        b_ref[rows, :] = (pieces[:, 0:GLA_DK] + pieces[:, GLA_DK:2 * GLA_DK]) + pieces[:, 2 * GLA_DK:3 * GLA_DK]

    for s, rows in enumerate(blocks):
        b = b_ref[rows, :]
        b_last = jnp.concatenate(
            [jnp.broadcast_to(b[(cc + 1) * c - 1:(cc + 1) * c, :], (c, GLA_DK)) for cc in range(cps)], axis=0)
        k = k_ref[rows, :]
        qd_ref[rows, :] = ((q_ref[rows, :] * q_scale) * jnp.exp(b)).astype(BF16)
        kd_ref[rows, :] = (k * jnp.exp(-b)).astype(BF16)
        ke_ref[rows, :] = (k * jnp.exp(b_last - b)).astype(BF16)
        chunk_decay = jnp.exp(b_last)
        for cc in range(cps):
            dec_ref[s * cps + cc] = chunk_decay[cc * c:cc * c + 8, :]

    def intra(part):
        for rows in part:
            at_ref[rows, :] = jnp.where(causal, _dot_nt(qd_ref[rows, :], kd_ref[rows, :]), 0.0).astype(BF16)
        for rows in part:
            for n in range(rows.start // c, rows.stop // c):
                u_ref[n] = _dot_tn(v_ref[n * c:(n + 1) * c, :], ke_ref[n * c:(n + 1) * c, :])
        for rows in part:
            oi_ref[rows, :] = _dot(at_ref[rows, :], v_ref[rows, :])

    def scan(part, st):
        for n in range(part[0].start // c, part[-1].stop // c):
            sb_ref[n] = st.astype(BF16)
            st = st * dec_ref[n, 0:1, :] + u_ref[n]
        return st

    def finish(part):
        for rows in part:
            for n in range(rows.start // c, rows.stop // c):
                crow = slice(n * c, (n + 1) * c)
                oi_ref[crow, :] = oi_ref[crow, :] + _dot_nt(qd_ref[crow, :], sb_ref[n])
        for rows in part:
            o = oi_ref[rows, :]
            ms = jnp.mean(o * o, axis=-1, keepdims=True)
            r = r_ref[rows, :]
            out = (o * lax.rsqrt(ms + EPS) * gn) * (r * _sigmoid(r))
            o_ref[rows, :] = out.astype(BF16)

    per_part = nsb // GLA_PARTS
    parts = [blocks[p * per_part:(p + 1) * per_part] for p in range(GLA_PARTS)]
    state = jnp.zeros((GLA_DV, GLA_DK), F32)
    intra(parts[0])
    for p in range(GLA_PARTS):
        state = scan(parts[p], state)
        if p + 1 < GLA_PARTS:
            intra(parts[p + 1])
        finish(parts[p])


def _gla(pf3, pb3, wlr, bforget, gnorm, layer, *, fcols, bcols):
    bsz, seq, _ = pf3.shape
    qa0, ka0, ra0, alr0 = (fcols[k][0] for k in ("qa", "ka", "ra", "alr"))
    va0 = bcols["va"][0]
    kern = functools.partial(_gla_kernel, seq=seq)
    return pl.pallas_call(
        kern,
        grid=(bsz, GLA_HEADS),
        in_specs=[
            pl.BlockSpec((None, seq, GLA_DK), lambda b, h: (b, 0, qa0 // GLA_DK + h)),
            pl.BlockSpec((None, seq, GLA_DK), lambda b, h: (b, 0, ka0 // GLA_DK + h)),
            pl.BlockSpec((None, seq, GLA_DV), lambda b, h: (b, 0, va0 // GLA_DV + h)),
            pl.BlockSpec((None, seq, GLA_DV), lambda b, h: (b, 0, ra0 // GLA_DV + h)),
            pl.BlockSpec((None, seq, ALR_PAD), lambda b, h: (b, 0, alr0 // ALR_PAD)),
            pl.BlockSpec((None, ALR_PAD, GLA_DK), lambda b, h: (layer, 0, h)),
            pl.BlockSpec((None, 1, GLA_DK), lambda b, h: (layer, 0, h)),
            pl.BlockSpec((None, 1, GLA_DV), lambda b, h: (layer, 0, h)),
        ],
        out_specs=pl.BlockSpec((None, seq, GLA_DV), lambda b, h: (b, 0, h)),
        out_shape=jax.ShapeDtypeStruct((bsz, seq, GLA_V_W), BF16),
        scratch_shapes=[
            pltpu.VMEM((seq, GLA_DK), BF16),
            pltpu.VMEM((seq // GLA_CHUNK, GLA_DV, GLA_DK), F32),
            pltpu.VMEM((seq // GLA_CHUNK, 8, GLA_DK), F32),
            pltpu.VMEM((seq, GLA_DV), F32),
            pltpu.VMEM((seq // GLA_CHUNK, GLA_DV, GLA_DK), BF16),
            pltpu.VMEM((seq, GLA_DK), F32),
            pltpu.VMEM((seq, GLA_DK), BF16),
            pltpu.VMEM((seq, GLA_DK), BF16),
            pltpu.VMEM((seq, GLA_SUPER), BF16),
            pltpu.VMEM((seq, 3 * GLA_DK), BF16),
        ],
        compiler_params=_params("parallel", "parallel"),
        name="gla",
    )(pf3, pf3, pb3, pf3, pf3, wlr, bforget, gnorm)


def _t5_bucket(rel):
    n = jnp.maximum(rel, 0)
    max_exact = REL_BUCKETS // 2
    nf = jnp.maximum(n, 1).astype(F32)
    large = max_exact + (jnp.log(nf / max_exact) / math.log(REL_MAX_DIST / max_exact)
                         * (REL_BUCKETS - max_exact)).astype(jnp.int32)
    large = jnp.minimum(large, REL_BUCKETS - 1)
    return jnp.where(n < max_exact, n, large)


def _far_bucket(seq):
    d = np.arange(MOBA_BLOCK + 1, max(seq, MOBA_BLOCK + 2), dtype=np.float64)
    max_exact = REL_BUCKETS // 2
    large = max_exact + np.floor(np.log(d / max_exact) / math.log(REL_MAX_DIST / max_exact)
                                 * (REL_BUCKETS - max_exact) + 1e-6)
    assert MOBA_BLOCK + 1 >= max_exact and np.all(large >= REL_BUCKETS - 1), "far blocks must share one bucket"
    return REL_BUCKETS - 1


def _bias_kernel(bucket_ref, rb_ref, o_ref, *, far_bucket):
    h = pl.program_id(0)
    blk = MOBA_BLOCK
    far = rb_ref[far_bucket, h]
    row = lax.broadcasted_iota(jnp.int32, (blk, blk), 0)
    col = lax.broadcasted_iota(jnp.int32, (blk, blk), 1)
    for t in range(2):
        bk = bucket_ref[t]
        acc = jnp.zeros((blk, blk), F32)
        for b in range(REL_BUCKETS):
            acc = jnp.where(bk == b, rb_ref[b, h], acc)
        acc = acc - far
        if t == 1:
            acc = jnp.where(col >= row, acc, NEG_INF)
        o_ref[t] = acc


def _bias_tiles(rel_bias, seq):
    blk = MOBA_BLOCK
    kpos = jnp.arange(blk, dtype=jnp.int32)[:, None]
    qpos = jnp.arange(blk, dtype=jnp.int32)[None, :]
    bucket = jnp.stack([_t5_bucket(qpos + blk - kpos), _t5_bucket(qpos - kpos)])
    kern = functools.partial(_bias_kernel, far_bucket=_far_bucket(seq))
    return pl.pallas_call(
        kern,
        grid=(MOBA_HEADS,),
        in_specs=[
            pl.BlockSpec((2, blk, blk), lambda h: (0, 0, 0)),
            pl.BlockSpec(memory_space=pltpu.SMEM),
        ],
        out_specs=pl.BlockSpec((None, 2, blk, blk), lambda h: (h, 0, 0, 0)),
        out_shape=jax.ShapeDtypeStruct((MOBA_HEADS, 2, blk, blk), F32),
        compiler_params=_params("arbitrary"),
        name="rel_bias_tiles",
    )(bucket, rel_bias)


PEN_ROWS = 16
DENOM_ROWS = 16


def _moba_kernel(qt_ref, k_ref, vt_ref, bias_ref, o_ref, kaug_ref, kmean_ref, qaug_ref, acc_ref, s_ref, *, nb):
    i = pl.program_id(1)
    blk, dh, heads = MOBA_BLOCK, MOBA_DH, MOBA_HEADS
    seq = nb * blk
    pair_w = 2 * dh
    assert pair_w == LANES and nb <= PEN_ROWS <= dh
    blk_shift = blk.bit_length() - 1
    assert (1 << blk_shift) == blk
    lane = lax.broadcasted_iota(jnp.int32, (1, pair_w), 1)

    @pl.when(i == 0)
    def _build_keys():
        rowblk = lax.broadcasted_iota(jnp.int32, (seq, pair_w), 0) >> blk_shift
        lane_s = lax.broadcasted_iota(jnp.int32, (seq, pair_w), 1)
        for p in range(heads // 2):
            kp = k_ref[:, p * pair_w:(p + 1) * pair_w]
            km = jnp.sum(kp.astype(F32).reshape(nb, blk, pair_w), axis=1) * (1.0 / blk)
            kmean_ref[p] = km
            for e in range(2):
                own = (lane_s >= dh) if e else (lane_s < dh)
                onehot = (lane_s - (0 if e else dh)) == rowblk
                kaug_ref[2 * p + e] = jnp.where(own, kp, onehot.astype(BF16))

    row8 = lax.broadcasted_iota(jnp.int32, (nb, blk), 0)
    scale = jnp.asarray(dh ** -0.5, BF16)

    gates = []
    for p in range(heads // 2):
        qtp = qt_ref[p * pair_w:(p + 1) * pair_w, :]
        km = kmean_ref[p]
        for e in range(2):
            own = (lane >= dh) if e else (lane < dh)
            km_hi, km_lo = _split_bf16(jnp.where(own, km, 0.0), 2)
            gates.append(_dot(km_hi, qtp) + _dot(km_lo, qtp))
    for p in range(heads // 2):
        qtp = qt_ref[p * pair_w:(p + 1) * pair_w, :]
        for e in range(2):
            g = jnp.where(row8 < i, gates[2 * p + e], -jnp.inf)
            rank = jnp.zeros((nb, blk), jnp.int32)
            for m in range(nb):
                gm = g[m:m + 1, :]
                beats = (gm > g) | ((gm == g) & (m < row8))
                rank = rank + beats.astype(jnp.int32)
            keep = ((rank < MOBA_TOPK) & (row8 < i)) | (row8 == i)
            pen_t = jnp.where(keep, 0.0, NEG_INF)
            pen = jnp.concatenate([pen_t, jnp.zeros((PEN_ROWS - nb, blk), F32)], axis=0).astype(BF16)
            qs = qtp[e * dh:(e + 1) * dh, :] * scale
            if e == 0:
                parts = [qs, pen, jnp.zeros((pair_w - dh - PEN_ROWS, blk), BF16)]
            else:
                parts = [pen, jnp.zeros((dh - PEN_ROWS, blk), BF16), qs]
            qaug_ref[2 * p + e] = jnp.concatenate(parts, axis=0)

    def scores(j, nblocks, slot):
        col_max = []
        for h in range(heads):
            s = _dot(kaug_ref[h, j * blk:(j + 1) * blk, :], qaug_ref[h])
            if j >= nblocks - 2:
                s = s + bias_ref[h, j - (nblocks - 2)]
            s_ref[slot, h] = s
            col_max.append(jnp.max(s, axis=0, keepdims=True))
        return col_max

    ones_rows = jnp.ones((DENOM_ROWS, blk), BF16)

    def attend(nblocks):
        col_max = scores(0, nblocks, 0)
        ms = [None] * heads
        for j in range(nblocks):
            next_max = scores(j + 1, nblocks, (j + 1) % 2) if j + 1 < nblocks else None
            for h in range(heads):
                m_new = col_max[h] if j == 0 else jnp.maximum(ms[h], col_max[h])
                pr = jnp.exp(s_ref[j % 2, h] - m_new).astype(BF16)
                vt_ones = jnp.concatenate([vt_ref[j, h * dh:(h + 1) * dh, :], ones_rows], axis=0)
                pv = _dot(vt_ones, pr)
                if j == 0:
                    acc_ref[h] = pv
                else:
                    acc_ref[h] = jnp.exp(ms[h] - m_new) * acc_ref[h] + pv
                ms[h] = m_new
            col_max = next_max
        for h in range(heads):
            o_ref[h * dh:(h + 1) * dh, :] = (acc_ref[h, 0:dh, :] / acc_ref[h, dh:dh + 1, :]).astype(BF16)

    for nblocks in range(1, nb + 1):
        pl.when(i == nblocks - 1)(functools.partial(attend, nblocks))


def _moba(qt, pb3, vt, bias, *, bcols):
    bsz, seq, _ = pb3.shape
    blk = MOBA_BLOCK
    nb = seq // blk
    assert seq % blk == 0
    kb0 = bcols["kb"][0]
    kern = functools.partial(_moba_kernel, nb=nb)
    return pl.pallas_call(
        kern,
        grid=(bsz, nb),
        in_specs=[
            pl.BlockSpec((None, None, MOBA_W, blk), lambda b, i: (b, i, 0, 0)),
            pl.BlockSpec((None, seq, MOBA_W), lambda b, i: (b, 0, kb0 // MOBA_W)),
            pl.BlockSpec((None, nb, MOBA_W, blk), lambda b, i: (b, 0, 0, 0)),
            _resident((MOBA_HEADS, 2, blk, blk), lambda b, i: (0, 0, 0, 0)),
        ],
        out_specs=pl.BlockSpec((None, None, MOBA_W, blk), lambda b, i: (b, i, 0, 0)),
        out_shape=jax.ShapeDtypeStruct((bsz, nb, MOBA_W, blk), BF16),
        scratch_shapes=[
            pltpu.VMEM((MOBA_HEADS, seq, 2 * MOBA_DH), BF16),
            pltpu.VMEM((MOBA_HEADS // 2, nb, 2 * MOBA_DH), F32),
            pltpu.VMEM((MOBA_HEADS, 2 * MOBA_DH, blk), BF16),
            pltpu.VMEM((MOBA_HEADS, MOBA_DH + DENOM_ROWS, blk), F32),
            pltpu.VMEM((2, MOBA_HEADS, blk, blk), F32),
        ],
        compiler_params=_params("arbitrary", "arbitrary"),
        name="moba",
    )(qt, pb3, vt, bias)


HALO = 8


def _mix_ffn_kernel(oa_ref, obt_ref, gate_ref, x_ref, wg_ref, wm_ref, wo_ref, g_ref, wu_ref, cw_ref, cb_ref,
                    wd_ref, gf_ref, o_ref, carry_ref, act_ref, *, d_ff, cw, kgroup, final):
    t = pl.program_id(1)
    tm, d = x_ref.shape

    @pl.when(t == 0)
    def _zero_history():
        carry_ref[...] = jnp.zeros_like(carry_ref)

    ya = _dot(oa_ref[...], wg_ref[...])
    wm = wm_ref[...]
    yb = jnp.concatenate([_dot_tn(obt_ref[s], wm) for s in range(obt_ref.shape[0])], axis=0)
    mixed = _sigmoid(gate_ref[:, 0:d]) * ya + _sigmoid(gate_ref[:, d:2 * d]) * yb
    x = x_ref[...] + _dot(mixed.astype(BF16), wo_ref[...])

    ms = jnp.mean(x * x, axis=-1, keepdims=True)
    h = (x * lax.rsqrt(ms + EPS) * g_ref[...]).astype(BF16)

    row_h = lax.broadcasted_iota(jnp.int32, (HALO, cw), 0)

    def shifted(u, prev, s):
        rolled = pltpu.roll(u, s, axis=0)
        head = jnp.where(row_h < s, pltpu.roll(prev, s, axis=0), rolled[0:HALO, :])
        return jnp.concatenate([head, rolled[HALO:, :]], axis=0)

    y = x
    nchunks = d_ff // cw
    for c in range(nchunks):
        halves = []
        for part in range(2):
            c0 = part * d_ff + c * cw
            u = _dot(h, wu_ref[:, c0:c0 + cw])
            prev = carry_ref[:, c0:c0 + cw]
            carry_ref[:, c0:c0 + cw] = u[tm - HALO:tm, :]
            w = cw_ref[:, c0:c0 + cw]
            conv = cb_ref[:, c0:c0 + cw] + w[CONV_W - 1:CONV_W, :] * u
            for s in range(1, CONV_W):
                conv = conv + w[CONV_W - 1 - s:CONV_W - s, :] * shifted(u, prev, s)
            halves.append(conv)
        a, bval = halves
        act_ref[:, c * cw:(c + 1) * cw] = ((a * _sigmoid(a)) * bval).astype(BF16)
        if (c + 1) % kgroup == 0 or c == nchunks - 1:
            k0 = (c // kgroup) * kgroup * cw
            k1 = (c + 1) * cw
            y = y + _dot(act_ref[:, k0:k1], wd_ref[k0:k1, :])
    if final:
        ms2 = jnp.mean(y * y, axis=-1, keepdims=True)
        y = y * lax.rsqrt(ms2 + EPS) * gf_ref[...]
    o_ref[...] = y


def _mix_ffn(oa3, obt, pf3, x3, wg, wm, wo, g, wu, cw_, cb, wd, gfinal, layer, *, tm, final):
    bsz, seq, d = x3.shape
    d_ff = wd.shape[1]
    blk = MOBA_BLOCK
    cw = 256
    assert d_ff % cw == 0 and seq % tm == 0 and tm % blk == 0
    kern = functools.partial(_mix_ffn_kernel, d_ff=d_ff, cw=cw, kgroup=4, final=final)
    return pl.pallas_call(
        kern,
        grid=(bsz, seq // tm),
        in_specs=[
            pl.BlockSpec((None, tm, GLA_V_W), lambda b, t: (b, t, 0)),
            pl.BlockSpec((None, tm // blk, MOBA_W, blk), lambda b, t: (b, t, 0, 0)),
            pl.BlockSpec((None, tm, D_MERGE * d), lambda b, t: (b, t, 0)),
            pl.BlockSpec((None, tm, d), lambda b, t: (b, t, 0)),
            _resident((None, GLA_V_W, d), lambda b, t: (layer, 0, 0)),
            _resident((None, MOBA_W, d), lambda b, t: (layer, 0, 0)),
            _resident((None, d, d), lambda b, t: (layer, 0, 0)),
            pl.BlockSpec((None, 1, d), lambda b, t: (layer, 0, 0)),
            _resident((None, d, 2 * d_ff), lambda b, t: (layer, 0, 0)),
            pl.BlockSpec((None, CONV_W, 2 * d_ff), lambda b, t: (layer, 0, 0)),
            pl.BlockSpec((None, 1, 2 * d_ff), lambda b, t: (layer, 0, 0)),
            _resident((None, d_ff, d), lambda b, t: (layer, 0, 0)),
            pl.BlockSpec((1, d), lambda b, t: (0, 0)),
        ],
        out_specs=pl.BlockSpec((None, tm, d), lambda b, t: (b, t, 0)),
        out_shape=jax.ShapeDtypeStruct((bsz, seq, d), F32),
        scratch_shapes=[
            pltpu.VMEM((HALO, 2 * d_ff), F32),
            pltpu.VMEM((tm, d_ff), BF16),
        ],
        compiler_params=_params("arbitrary", "arbitrary"),
        name="mix_ffn",
    )(oa3, obt, pf3, x3, wg, wm, wo, g, wu, cw_, cb, wd, gfinal)


def kernel(x, rel_bias, norm_mix, w_in, w_lr_up, b_forget, gla_out_norm, w_branch_gla, w_branch_moba,
           w_out, norm_ffn, w_up, conv_w, conv_b, w_down, norm_final):
    bsz, seq, d = x.shape
    depth = w_in.shape[0]
    n = bsz * seq
    fcols, wf, bcols, wb = _proj_layout(d)

    w_in_rows, w_in_t = _split_w_in(w_in, d, fcols, bcols)
    wlr = jnp.pad(w_lr_up, ((0, 0), (0, ALR_PAD - GLA_RANK), (0, 0)))
    wg = w_branch_gla.astype(BF16)
    wm = w_branch_moba.astype(BF16)
    wo = w_out.astype(BF16)
    wu = w_up.astype(BF16)
    wd = w_down.astype(BF16)
    norm_mix3 = norm_mix[:, None, :]
    norm_ffn3 = norm_ffn[:, None, :]
    b_forget3 = b_forget[:, None, :]
    gnorm3 = gla_out_norm[:, None, :]
    conv_b3 = conv_b[:, None, :]
    gfinal = norm_final[None, :]

    bias = _bias_tiles(rel_bias, seq)

    for l in range(depth):
        pf, pb, qt, vt = _in_proj(x.reshape(n, d), norm_mix3, w_in_rows, w_in_t, l,
                                  seq=seq, n_f32=len(fcols), wf=wf, wb=wb, tm=512)
        pf3 = pf.reshape(bsz, seq, wf)
        pb3 = pb.reshape(bsz, seq, wb)
        oa = _gla(pf3, pb3, wlr, b_forget3, gnorm3, l, fcols=fcols, bcols=bcols)
        obt = _moba(qt, pb3, vt, bias, bcols=bcols)
        assert fcols["gates"][0] == 0
        x = _mix_ffn(oa, obt, pf3, x, wg, wm, wo, norm_ffn3, wu, conv_w, conv_b3, wd, gfinal, l,
                     tm=512, final=(l == depth - 1))
    return x
```

```python
import functools
import math

import jax
import jax.numpy as jnp
import numpy as np
from jax import lax
from jax.experimental import pallas as pl
from jax.experimental.pallas import tpu as pltpu

GLA_HEADS = 4
GLA_DK = 128
GLA_DV = 256
GLA_RANK = 16
GLA_TAU = 16.0
GLA_CHUNK = 64
MOBA_HEADS = 8
MOBA_DH = 64
MOBA_BLOCK = 256
MOBA_TOPK = 3
REL_BUCKETS = 32
REL_MAX_DIST = 128
CONV_W = 3
EPS = 1e-6
NEG_INF = -1e30

GLA_QK_W = GLA_HEADS * GLA_DK
GLA_V_W = GLA_HEADS * GLA_DV
MOBA_W = MOBA_HEADS * MOBA_DH

LANES = 128
VMEM_LIMIT_BYTES = 56 * 1024 * 1024

F32 = jnp.float32
BF16 = jnp.bfloat16
HIGHEST = lax.Precision.HIGHEST

_NT = (((1,), (1,)), ((), ()))
_TN = (((0,), (0,)), ((), ()))


def _dot(a, b, precision=None):
    return jnp.dot(a, b, preferred_element_type=F32, precision=precision)


def _dot_nt(a, b, precision=None):
    return lax.dot_general(a, b, _NT, preferred_element_type=F32, precision=precision)


def _dot_tn(a, b):
    return lax.dot_general(a, b, _TN, preferred_element_type=F32)


def _split_bf16(x, n):
    parts = []
    for _ in range(n - 1):
        p = x.astype(BF16)
        parts.append(p)
        x = x - p.astype(F32)
    parts.append(x.astype(BF16))
    return parts


def _sigmoid(x):
    return 1.0 / (1.0 + jnp.exp(-x))


def _params(*semantics):
    return pltpu.CompilerParams(dimension_semantics=semantics, vmem_limit_bytes=VMEM_LIMIT_BYTES)


def _resident(shape, index_map):
    return pl.BlockSpec(shape, index_map, pipeline_mode=pl.Buffered(1))


D_MERGE = 2
ALR_PAD = LANES


def _proj_layout(d_model):
    f32_cols = dict(gates=(0, D_MERGE * d_model))
    off = D_MERGE * d_model
    for name, w in (("ra", GLA_V_W), ("qa", GLA_QK_W), ("ka", GLA_QK_W), ("alr", ALR_PAD)):
        f32_cols[name] = (off, w)
        off += w
    wf = off
    bf_cols = {}
    off = 0
    for name, w in (("va", GLA_V_W), ("kb", MOBA_W)):
        bf_cols[name] = (off, w)
        off += w
    return f32_cols, wf, bf_cols, off


def _split_w_in(w_in, d_model, f32_names, bf16_names):
    splits = (GLA_QK_W, GLA_QK_W, GLA_V_W, GLA_V_W, GLA_RANK, MOBA_W, MOBA_W, MOBA_W, D_MERGE * d_model)
    pts = np.cumsum(splits)[:-1].tolist()
    pieces = dict(zip(("qa", "ka", "va", "ra", "alr", "qb", "kb", "vb", "gates"), jnp.split(w_in, pts, axis=-1)))
    pieces["alr"] = jnp.pad(pieces["alr"], ((0, 0), (0, 0), (0, ALR_PAD - GLA_RANK)))
    rows = [pieces[k].astype(BF16) for k in tuple(f32_names) + tuple(bf16_names)]
    transposed = [jnp.swapaxes(pieces[k], 1, 2).astype(BF16) for k in ("qb", "vb")]
    return rows, transposed


def _in_proj_kernel(x_ref, g_ref, *refs, n_f32, n_bf16, chunk):
    w_f32 = refs[:n_f32]
    w_bf16 = refs[n_f32:n_f32 + n_bf16]
    wqt_ref, wvt_ref, of_ref, ob_ref, qt_ref, vt_ref = refs[n_f32 + n_bf16:]
    x = x_ref[...]
    ms = jnp.mean(x * x, axis=-1, keepdims=True)
    h = (x * lax.rsqrt(ms + EPS) * g_ref[...]).astype(BF16)
    for w_refs, out_ref in ((w_f32, of_ref), (w_bf16, ob_ref)):
        off = 0
        for w_ref in w_refs:
            width = w_ref.shape[1]
            for c0 in range(0, width, chunk):
                c1 = min(c0 + chunk, width)
                out_ref[:, off + c0:off + c1] = _dot(h, w_ref[:, c0:c1]).astype(out_ref.dtype)
            off += width
    blk = MOBA_BLOCK
    for s in range(x.shape[0] // blk):
        hs = h[s * blk:(s + 1) * blk, :]
        qt_ref[s] = _dot_nt(wqt_ref[...], hs).astype(BF16)
        vt_ref[s] = _dot_nt(wvt_ref[...], hs).astype(BF16)


def _in_proj(x2, g, w_rows, w_t, layer, *, seq, n_f32, wf, wb, tm):
    n, d = x2.shape
    blk = MOBA_BLOCK
    tpb = seq // tm
    spt = tm // blk
    kern = functools.partial(_in_proj_kernel, n_f32=n_f32, n_bf16=len(w_rows) - n_f32, chunk=512)
    t_shape = jax.ShapeDtypeStruct((n // seq, seq // blk, MOBA_W, blk), BF16)
    t_spec = pl.BlockSpec((None, spt, MOBA_W, blk), lambda i: (i // tpb, i % tpb, 0, 0))
    w_specs = [_resident((None,) + w.shape[1:], lambda i: (layer, 0, 0)) for w in tuple(w_rows) + tuple(w_t)]
    return pl.pallas_call(
        kern,
        grid=(n // tm,),
        in_specs=[
            pl.BlockSpec((tm, d), lambda i: (i, 0)),
            pl.BlockSpec((None, 1, d), lambda i: (layer, 0, 0)),
            *w_specs,
        ],
        out_specs=[
            pl.BlockSpec((tm, wf), lambda i: (i, 0)),
            pl.BlockSpec((tm, wb), lambda i: (i, 0)),
            t_spec,
            t_spec,
        ],
        out_shape=[jax.ShapeDtypeStruct((n, wf), F32), jax.ShapeDtypeStruct((n, wb), BF16), t_shape, t_shape],
        compiler_params=_params("parallel"),
        name="in_proj",
    )(x2, g, *w_rows, *w_t)


GLA_SUPER = 256
GLA_PARTS = 4


def _unpack_ride(rest, ride):
    if not ride:
        return rest[0], rest[1:]
    ride_in, out, ride_out = rest[:3]
    ride_out[...] = ride_in[...].astype(BF16)
    return out, rest[3:]


def _ride_specs(ride, steps, step_index):
    rows, cols = ride.shape
    assert rows % steps == 0 and (rows // steps) % 16 == 0
    spec = pl.BlockSpec((rows // steps, cols), lambda *g: (step_index(*g), 0))
    return spec, jax.ShapeDtypeStruct((rows, cols), BF16)


def _gla_kernel(q_ref, k_ref, v_ref, r_ref, alr_ref, wlr_ref, bf_ref, gn_ref, *rest, seq, ride):
    o_ref, (qd_ref, u_ref, dec_ref, oi_ref, sb_ref, b_ref, kd_ref, ke_ref, at_ref, la_ref) = _unpack_ride(rest, ride)
    c, sb = GLA_CHUNK, GLA_SUPER
    cps = sb // c
    nsb = seq // sb
    nchunks = seq // c
    shift = c.bit_length() - 1
    assert (1 << shift) == c and seq % sb == 0
    row = lax.broadcasted_iota(jnp.int32, (sb, sb), 0)
    col = lax.broadcasted_iota(jnp.int32, (sb, sb), 1)
    same_chunk = (row >> shift) == (col >> shift)
    causal = same_chunk & (row >= col)
    tril = causal.astype(BF16)
    wlr_parts = _split_bf16(wlr_ref[...], 2)
    bfg = bf_ref[...]
    gn = gn_ref[...]
    q_scale = GLA_DK ** -0.5

    blocks = [slice(s * sb, (s + 1) * sb) for s in range(nsb)]

    for rows in blocks:
        a_hi, a_lo = _split_bf16(alr_ref[rows, :], 2)
        xa = _dot(a_hi, wlr_parts[0]) + _dot(a_hi, wlr_parts[1]) + _dot(a_lo, wlr_parts[0]) + bfg
        log_a = (jnp.minimum(xa, 0.0) - jnp.log1p(jnp.exp(-jnp.abs(xa)))) * (1.0 / GLA_TAU)
        la_ref[rows, :] = jnp.concatenate(_split_bf16(log_a, 3), axis=1)

    for rows in blocks:
        pieces = _dot(tril, la_ref[rows, :])
        b_ref[rows, :] = (pieces[:, 0:GLA_DK] + pieces[:, GLA_DK:2 * GLA_DK]) + pieces[:, 2 * GLA_DK:3 * GLA_DK]

    for s, rows in enumerate(blocks):
        b = b_ref[rows, :]
        b_last = jnp.concatenate(
            [jnp.broadcast_to(b[(cc + 1) * c - 1:(cc + 1) * c, :], (c, GLA_DK)) for cc in range(cps)], axis=0)
        k = k_ref[rows, :]
        qd_ref[rows, :] = ((q_ref[rows, :] * q_scale) * jnp.exp(b)).astype(BF16)
        kd_ref[rows, :] = (k * jnp.exp(-b)).astype(BF16)
        ke_ref[rows, :] = (k * jnp.exp(b_last - b)).astype(BF16)
        chunk_decay = jnp.exp(b_last)
        for cc in range(cps):
            dec_ref[s * cps + cc] = chunk_decay[cc * c:cc * c + 8, :]

    def intra(part):
        for rows in part:
            at_ref[rows, :] = jnp.where(causal, _dot_nt(qd_ref[rows, :], kd_ref[rows, :]), 0.0).astype(BF16)
        for rows in part:
            for n in range(rows.start // c, rows.stop // c):
                u_ref[n] = _dot_tn(v_ref[n * c:(n + 1) * c, :], ke_ref[n * c:(n + 1) * c, :])
        for rows in part:
            oi_ref[rows, :] = _dot(at_ref[rows, :], v_ref[rows, :])

    def scan(part, st):
        for n in range(part[0].start // c, part[-1].stop // c):
            sb_ref[n] = st.astype(BF16)
            st = st * dec_ref[n, 0:1, :] + u_ref[n]
        return st

    def finish(part):
        for rows in part:
            for n in range(rows.start // c, rows.stop // c):
                crow = slice(n * c, (n + 1) * c)
                oi_ref[crow, :] = oi_ref[crow, :] + _dot_nt(qd_ref[crow, :], sb_ref[n])
        for rows in part:
            o = oi_ref[rows, :]
            ms = jnp.mean(o * o, axis=-1, keepdims=True)
            r = r_ref[rows, :]
            out = (o * lax.rsqrt(ms + EPS) * gn) * (r * _sigmoid(r))
            o_ref[rows, :] = out.astype(BF16)

    per_part = nsb // GLA_PARTS
    parts = [blocks[p * per_part:(p + 1) * per_part] for p in range(GLA_PARTS)]
    state = jnp.zeros((GLA_DV, GLA_DK), F32)
    intra(parts[0])
    for p in range(GLA_PARTS):
        state = scan(parts[p], state)
        if p + 1 < GLA_PARTS:
            intra(parts[p + 1])
        finish(parts[p])


def _gla(pf3, pb3, wlr, bforget, gnorm, layer, *, fcols, bcols, ride=None):
    bsz, seq, _ = pf3.shape
    qa0, ka0, ra0, alr0 = (fcols[k][0] for k in ("qa", "ka", "ra", "alr"))
    va0 = bcols["va"][0]
    kern = functools.partial(_gla_kernel, seq=seq, ride=ride is not None)
    ride_in, ride_out_spec, ride_out_shape, ride_args = [], [], [], []
    if ride is not None:
        spec, shape = _ride_specs(ride, bsz * GLA_HEADS, lambda b, h: b * GLA_HEADS + h)
        ride_in, ride_out_spec, ride_out_shape, ride_args = [spec], [spec], [shape], [ride]
    outs = pl.pallas_call(
        kern,
        grid=(bsz, GLA_HEADS),
        in_specs=[
            pl.BlockSpec((None, seq, GLA_DK), lambda b, h: (b, 0, qa0 // GLA_DK + h)),
            pl.BlockSpec((None, seq, GLA_DK), lambda b, h: (b, 0, ka0 // GLA_DK + h)),
            pl.BlockSpec((None, seq, GLA_DV), lambda b, h: (b, 0, va0 // GLA_DV + h)),
            pl.BlockSpec((None, seq, GLA_DV), lambda b, h: (b, 0, ra0 // GLA_DV + h)),
            pl.BlockSpec((None, seq, ALR_PAD), lambda b, h: (b, 0, alr0 // ALR_PAD)),
            pl.BlockSpec((None, ALR_PAD, GLA_DK), lambda b, h: (layer, 0, h)),
            pl.BlockSpec((None, 1, GLA_DK), lambda b, h: (layer, 0, h)),
            pl.BlockSpec((None, 1, GLA_DV), lambda b, h: (layer, 0, h)),
        ] + ride_in,
        out_specs=[pl.BlockSpec((None, seq, GLA_DV), lambda b, h: (b, 0, h))] + ride_out_spec,
        out_shape=[jax.ShapeDtypeStruct((bsz, seq, GLA_V_W), BF16)] + ride_out_shape,
        scratch_shapes=[
            pltpu.VMEM((seq, GLA_DK), BF16),
            pltpu.VMEM((seq // GLA_CHUNK, GLA_DV, GLA_DK), F32),
            pltpu.VMEM((seq // GLA_CHUNK, 8, GLA_DK), F32),
            pltpu.VMEM((seq, GLA_DV), F32),
            pltpu.VMEM((seq // GLA_CHUNK, GLA_DV, GLA_DK), BF16),
            pltpu.VMEM((seq, GLA_DK), F32),
            pltpu.VMEM((seq, GLA_DK), BF16),
            pltpu.VMEM((seq, GLA_DK), BF16),
            pltpu.VMEM((seq, GLA_SUPER), BF16),
            pltpu.VMEM((seq, 3 * GLA_DK), BF16),
        ],
        compiler_params=_params("parallel", "parallel"),
        name="gla",
    )(pf3, pf3, pb3, pf3, pf3, wlr, bforget, gnorm, *ride_args)
    return outs if ride is not None else outs[0]


def _t5_bucket(rel):
    n = jnp.maximum(rel, 0)
    max_exact = REL_BUCKETS // 2
    nf = jnp.maximum(n, 1).astype(F32)
    large = max_exact + (jnp.log(nf / max_exact) / math.log(REL_MAX_DIST / max_exact)
                         * (REL_BUCKETS - max_exact)).astype(jnp.int32)
    large = jnp.minimum(large, REL_BUCKETS - 1)
    return jnp.where(n < max_exact, n, large)


def _far_bucket(seq):
    d = np.arange(MOBA_BLOCK + 1, max(seq, MOBA_BLOCK + 2), dtype=np.float64)
    max_exact = REL_BUCKETS // 2
    large = max_exact + np.floor(np.log(d / max_exact) / math.log(REL_MAX_DIST / max_exact)
                                 * (REL_BUCKETS - max_exact) + 1e-6)
    assert MOBA_BLOCK + 1 >= max_exact and np.all(large >= REL_BUCKETS - 1), "far blocks must share one bucket"
    return REL_BUCKETS - 1


def _bias_kernel(bucket_ref, rb_ref, o_ref, *, far_bucket):
    h = pl.program_id(0)
    blk = MOBA_BLOCK
    far = rb_ref[far_bucket, h]
    row = lax.broadcasted_iota(jnp.int32, (blk, blk), 0)
    col = lax.broadcasted_iota(jnp.int32, (blk, blk), 1)
    for t in range(2):
        bk = bucket_ref[t]
        acc = jnp.zeros((blk, blk), F32)
        for b in range(REL_BUCKETS):
            acc = jnp.where(bk == b, rb_ref[b, h], acc)
        acc = acc - far
        if t == 1:
            acc = jnp.where(col >= row, acc, NEG_INF)
        o_ref[t] = acc


def _bias_tiles(rel_bias, seq):
    blk = MOBA_BLOCK
    kpos = jnp.arange(blk, dtype=jnp.int32)[:, None]
    qpos = jnp.arange(blk, dtype=jnp.int32)[None, :]
    bucket = jnp.stack([_t5_bucket(qpos + blk - kpos), _t5_bucket(qpos - kpos)])
    kern = functools.partial(_bias_kernel, far_bucket=_far_bucket(seq))
    return pl.pallas_call(
        kern,
        grid=(MOBA_HEADS,),
        in_specs=[
            pl.BlockSpec((2, blk, blk), lambda h: (0, 0, 0)),
            pl.BlockSpec(memory_space=pltpu.SMEM),
        ],
        out_specs=pl.BlockSpec((None, 2, blk, blk), lambda h: (h, 0, 0, 0)),
        out_shape=jax.ShapeDtypeStruct((MOBA_HEADS, 2, blk, blk), F32),
        compiler_params=_params("arbitrary"),
        name="rel_bias_tiles",
    )(bucket, rel_bias)


PEN_ROWS = 16
DENOM_ROWS = 16


def _moba_kernel(qt_ref, k_ref, vt_ref, bias_ref, *rest, nb, ride):
    o_ref, (kaug_ref, kmean_ref, qaug_ref, acc_ref, s_ref) = _unpack_ride(rest, ride)
    i = pl.program_id(1)
    blk, dh, heads = MOBA_BLOCK, MOBA_DH, MOBA_HEADS
    seq = nb * blk
    pair_w = 2 * dh
    assert pair_w == LANES and nb <= PEN_ROWS <= dh
    blk_shift = blk.bit_length() - 1
    assert (1 << blk_shift) == blk
    lane = lax.broadcasted_iota(jnp.int32, (1, pair_w), 1)

    @pl.when(i == 0)
    def _build_keys():
        rowblk = lax.broadcasted_iota(jnp.int32, (seq, pair_w), 0) >> blk_shift
        lane_s = lax.broadcasted_iota(jnp.int32, (seq, pair_w), 1)
        for p in range(heads // 2):
            kp = k_ref[:, p * pair_w:(p + 1) * pair_w]
            km = jnp.sum(kp.astype(F32).reshape(nb, blk, pair_w), axis=1) * (1.0 / blk)
            kmean_ref[p] = km
            for e in range(2):
                own = (lane_s >= dh) if e else (lane_s < dh)
                onehot = (lane_s - (0 if e else dh)) == rowblk
                kaug_ref[2 * p + e] = jnp.where(own, kp, onehot.astype(BF16))

    row8 = lax.broadcasted_iota(jnp.int32, (nb, blk), 0)
    scale = jnp.asarray(dh ** -0.5, BF16)

    gates = []
    for p in range(heads // 2):
        qtp = qt_ref[p * pair_w:(p + 1) * pair_w, :]
        km = kmean_ref[p]
        for e in range(2):
            own = (lane >= dh) if e else (lane < dh)
            km_hi, km_lo = _split_bf16(jnp.where(own, km, 0.0), 2)
            gates.append(_dot(km_hi, qtp) + _dot(km_lo, qtp))
    for p in range(heads // 2):
        qtp = qt_ref[p * pair_w:(p + 1) * pair_w, :]
        for e in range(2):
            g = jnp.where(row8 < i, gates[2 * p + e], -jnp.inf)
            rank = jnp.zeros((nb, blk), jnp.int32)
            for m in range(nb):
                gm = g[m:m + 1, :]
                beats = (gm > g) | ((gm == g) & (m < row8))
                rank = rank + beats.astype(jnp.int32)
            keep = ((rank < MOBA_TOPK) & (row8 < i)) | (row8 == i)
            pen_t = jnp.where(keep, 0.0, NEG_INF)
            pen = jnp.concatenate([pen_t, jnp.zeros((PEN_ROWS - nb, blk), F32)], axis=0).astype(BF16)
            qs = qtp[e * dh:(e + 1) * dh, :] * scale
            if e == 0:
                parts = [qs, pen, jnp.zeros((pair_w - dh - PEN_ROWS, blk), BF16)]
            else:
                parts = [pen, jnp.zeros((dh - PEN_ROWS, blk), BF16), qs]
            qaug_ref[2 * p + e] = jnp.concatenate(parts, axis=0)

    def scores(j, nblocks, slot):
        col_max = []
        for h in range(heads):
            s = _dot(kaug_ref[h, j * blk:(j + 1) * blk, :], qaug_ref[h])
            if j >= nblocks - 2:
                s = s + bias_ref[h, j - (nblocks - 2)]
            s_ref[slot, h] = s
            col_max.append(jnp.max(s, axis=0, keepdims=True))
        return col_max

    ones_rows = jnp.ones((DENOM_ROWS, blk), BF16)

    def attend(nblocks):
        col_max = scores(0, nblocks, 0)
        ms = [None] * heads
        for j in range(nblocks):
            next_max = scores(j + 1, nblocks, (j + 1) % 2) if j + 1 < nblocks else None
            for h in range(heads):
                m_new = col_max[h] if j == 0 else jnp.maximum(ms[h], col_max[h])
                pr = jnp.exp(s_ref[j % 2, h] - m_new).astype(BF16)
                vt_ones = jnp.concatenate([vt_ref[j, h * dh:(h + 1) * dh, :], ones_rows], axis=0)
                pv = _dot(vt_ones, pr)
                if j == 0:
                    acc_ref[h] = pv
                else:
                    acc_ref[h] = jnp.exp(ms[h] - m_new) * acc_ref[h] + pv
                ms[h] = m_new
            col_max = next_max
        for h in range(heads):
            o_ref[h * dh:(h + 1) * dh, :] = (acc_ref[h, 0:dh, :] / acc_ref[h, dh:dh + 1, :]).astype(BF16)

    for nblocks in range(1, nb + 1):
        pl.when(i == nblocks - 1)(functools.partial(attend, nblocks))


def _moba(qt, pb3, vt, bias, *, bcols, ride=None):
    bsz, seq, _ = pb3.shape
    blk = MOBA_BLOCK
    nb = seq // blk
    assert seq % blk == 0
    kb0 = bcols["kb"][0]
    kern = functools.partial(_moba_kernel, nb=nb, ride=ride is not None)
    ride_in, ride_out_spec, ride_out_shape, ride_args = [], [], [], []
    if ride is not None:
        spec, shape = _ride_specs(ride, bsz * nb, lambda b, i: b * nb + i)
        ride_in, ride_out_spec, ride_out_shape, ride_args = [spec], [spec], [shape], [ride]
    outs = pl.pallas_call(
        kern,
        grid=(bsz, nb),
        in_specs=[
            pl.BlockSpec((None, None, MOBA_W, blk), lambda b, i: (b, i, 0, 0)),
            pl.BlockSpec((None, seq, MOBA_W), lambda b, i: (b, 0, kb0 // MOBA_W)),
            pl.BlockSpec((None, nb, MOBA_W, blk), lambda b, i: (b, 0, 0, 0)),
            _resident((MOBA_HEADS, 2, blk, blk), lambda b, i: (0, 0, 0, 0)),
        ] + ride_in,
        out_specs=[pl.BlockSpec((None, None, MOBA_W, blk), lambda b, i: (b, i, 0, 0))] + ride_out_spec,
        out_shape=[jax.ShapeDtypeStruct((bsz, nb, MOBA_W, blk), BF16)] + ride_out_shape,
        scratch_shapes=[
            pltpu.VMEM((MOBA_HEADS, seq, 2 * MOBA_DH), BF16),
            pltpu.VMEM((MOBA_HEADS // 2, nb, 2 * MOBA_DH), F32),
            pltpu.VMEM((MOBA_HEADS, 2 * MOBA_DH, blk), BF16),
            pltpu.VMEM((MOBA_HEADS, MOBA_DH + DENOM_ROWS, blk), F32),
            pltpu.VMEM((2, MOBA_HEADS, blk, blk), F32),
        ],
        compiler_params=_params("arbitrary", "arbitrary"),
        name="moba",
    )(qt, pb3, vt, bias, *ride_args)
    return outs if ride is not None else outs[0]


HALO = 8


def _mix_ffn_kernel(oa_ref, obt_ref, gate_ref, x_ref, wg_ref, wm_ref, wo_ref, g_ref, wu_ref, cw_ref, cb_ref,
                    wd_ref, gf_ref, o_ref, carry_ref, act_ref, *, d_ff, cw, kgroup, final):
    t = pl.program_id(1)
    tm, d = x_ref.shape

    @pl.when(t == 0)
    def _zero_history():
        carry_ref[...] = jnp.zeros_like(carry_ref)

    ya = _dot(oa_ref[...], wg_ref[...])
    wm = wm_ref[...]
    yb = jnp.concatenate([_dot_tn(obt_ref[s], wm) for s in range(obt_ref.shape[0])], axis=0)
    mixed = _sigmoid(gate_ref[:, 0:d]) * ya + _sigmoid(gate_ref[:, d:2 * d]) * yb
    x = x_ref[...] + _dot(mixed.astype(BF16), wo_ref[...])

    ms = jnp.mean(x * x, axis=-1, keepdims=True)
    h = (x * lax.rsqrt(ms + EPS) * g_ref[...]).astype(BF16)

    row_h = lax.broadcasted_iota(jnp.int32, (HALO, cw), 0)

    def shifted(u, prev, s):
        rolled = pltpu.roll(u, s, axis=0)
        head = jnp.where(row_h < s, pltpu.roll(prev, s, axis=0), rolled[0:HALO, :])
        return jnp.concatenate([head, rolled[HALO:, :]], axis=0)

    y = x
    nchunks = d_ff // cw
    lag = 1
    flushed = 0
    for c in range(nchunks):
        halves = []
        for part in range(2):
            c0 = part * d_ff + c * cw
            halves.append(_dot(h, wu_ref[:, c0:c0 + cw]))
        done = c - lag
        if done >= 0 and ((done + 1) % kgroup == 0):
            k0 = (done + 1 - kgroup) * cw
            k1 = (done + 1) * cw
            y = y + _dot(act_ref[:, k0:k1], wd_ref[k0:k1, :])
            flushed = done + 1
        for part in range(2):
            c0 = part * d_ff + c * cw
            u = halves[part]
            prev = carry_ref[:, c0:c0 + cw]
            carry_ref[:, c0:c0 + cw] = u[tm - HALO:tm, :]
            w = cw_ref[:, c0:c0 + cw]
            conv = cb_ref[:, c0:c0 + cw] + w[CONV_W - 1:CONV_W, :] * u
            for s in range(1, CONV_W):
                conv = conv + w[CONV_W - 1 - s:CONV_W - s, :] * shifted(u, prev, s)
            halves[part] = conv
        a, bval = halves
        act_ref[:, c * cw:(c + 1) * cw] = ((a * _sigmoid(a)) * bval).astype(BF16)
    if flushed < nchunks:
        y = y + _dot(act_ref[:, flushed * cw:d_ff], wd_ref[flushed * cw:d_ff, :])
    if final:
        ms2 = jnp.mean(y * y, axis=-1, keepdims=True)
        y = y * lax.rsqrt(ms2 + EPS) * gf_ref[...]
    o_ref[...] = y


def _mix_ffn(oa3, obt, pf3, x3, wg, wm, wo, g, wu, cw_, cb, wd, gfinal, layer, *, tm, final):
    bsz, seq, d = x3.shape
    d_ff = wd.shape[1]
    blk = MOBA_BLOCK
    cw = 256
    assert d_ff % cw == 0 and seq % tm == 0 and tm % blk == 0
    kern = functools.partial(_mix_ffn_kernel, d_ff=d_ff, cw=cw, kgroup=4, final=final)
    return pl.pallas_call(
        kern,
        grid=(bsz, seq // tm),
        in_specs=[
            pl.BlockSpec((None, tm, GLA_V_W), lambda b, t: (b, t, 0)),
            pl.BlockSpec((None, tm // blk, MOBA_W, blk), lambda b, t: (b, t, 0, 0)),
            pl.BlockSpec((None, tm, D_MERGE * d), lambda b, t: (b, t, 0)),
            pl.BlockSpec((None, tm, d), lambda b, t: (b, t, 0)),
            _resident((None, GLA_V_W, d), lambda b, t: (layer, 0, 0)),
            _resident((None, MOBA_W, d), lambda b, t: (layer, 0, 0)),
            _resident((None, d, d), lambda b, t: (layer, 0, 0)),
            pl.BlockSpec((None, 1, d), lambda b, t: (layer, 0, 0)),
            _resident((None, d, 2 * d_ff), lambda b, t: (layer, 0, 0)),
            pl.BlockSpec((None, CONV_W, 2 * d_ff), lambda b, t: (layer, 0, 0)),
            pl.BlockSpec((None, 1, 2 * d_ff), lambda b, t: (layer, 0, 0)),
            _resident((None, d_ff, d), lambda b, t: (layer, 0, 0)),
            pl.BlockSpec((1, d), lambda b, t: (0, 0)),
        ],
        out_specs=pl.BlockSpec((None, tm, d), lambda b, t: (b, t, 0)),
        out_shape=jax.ShapeDtypeStruct((bsz, seq, d), F32),
        scratch_shapes=[
            pltpu.VMEM((HALO, 2 * d_ff), F32),
            pltpu.VMEM((tm, d_ff), BF16),
        ],
        compiler_params=_params("arbitrary", "arbitrary"),
        name="mix_ffn",
    )(oa3, obt, pf3, x3, wg, wm, wo, g, wu, cw_, cb, wd, gfinal)


def kernel(x, rel_bias, norm_mix, w_in, w_lr_up, b_forget, gla_out_norm, w_branch_gla, w_branch_moba,
           w_out, norm_ffn, w_up, conv_w, conv_b, w_down, norm_final):
    bsz, seq, d = x.shape
    depth = w_in.shape[0]
    n = bsz * seq
    fcols, wf, bcols, wb = _proj_layout(d)

    w_in_rows, w_in_t = _split_w_in(w_in, d, fcols, bcols)
    wlr = jnp.pad(w_lr_up, ((0, 0), (0, ALR_PAD - GLA_RANK), (0, 0)))
    wg = w_branch_gla.astype(BF16)
    wm = w_branch_moba.astype(BF16)
    wo = w_out.astype(BF16)
    norm_mix3 = norm_mix[:, None, :]
    norm_ffn3 = norm_ffn[:, None, :]
    b_forget3 = b_forget[:, None, :]
    gnorm3 = gla_out_norm[:, None, :]
    conv_b3 = conv_b[:, None, :]
    gfinal = norm_final[None, :]

    bias = _bias_tiles(rel_bias, seq)

    for l in range(depth):
        pf, pb, qt, vt = _in_proj(x.reshape(n, d), norm_mix3, w_in_rows, w_in_t, l,
                                  seq=seq, n_f32=len(fcols), wf=wf, wb=wb, tm=512)
        pf3 = pf.reshape(bsz, seq, wf)
        pb3 = pb.reshape(bsz, seq, wb)
        if l == 0:
            oa, wu = _gla(pf3, pb3, wlr, b_forget3, gnorm3, l, fcols=fcols, bcols=bcols,
                          ride=w_up.reshape(depth * d, w_up.shape[2]))
            obt, wd = _moba(qt, pb3, vt, bias, bcols=bcols, ride=w_down.reshape(depth * w_down.shape[1], d))
            wu = wu.reshape(w_up.shape)
            wd = wd.reshape(w_down.shape)
        else:
            oa = _gla(pf3, pb3, wlr, b_forget3, gnorm3, l, fcols=fcols, bcols=bcols)
            obt = _moba(qt, pb3, vt, bias, bcols=bcols)
        assert fcols["gates"][0] == 0
        x = _mix_ffn(oa, obt, pf3, x, wg, wm, wo, norm_ffn3, wu, conv_w, conv_b3, wd, gfinal, l,
                     tm=512, final=(l == depth - 1))
    return x
```

```python
import functools
import math

import jax
import jax.numpy as jnp
import numpy as np
from jax import lax
from jax.experimental import pallas as pl
from jax.experimental.pallas import tpu as pltpu

GLA_HEADS = 4
GLA_DK = 128
GLA_DV = 256
GLA_RANK = 16
GLA_TAU = 16.0
GLA_CHUNK = 64
MOBA_HEADS = 8
MOBA_DH = 64
MOBA_BLOCK = 256
MOBA_TOPK = 3
REL_BUCKETS = 32
REL_MAX_DIST = 128
CONV_W = 3
EPS = 1e-6
NEG_INF = -1e30

GLA_QK_W = GLA_HEADS * GLA_DK
GLA_V_W = GLA_HEADS * GLA_DV
MOBA_W = MOBA_HEADS * MOBA_DH

LANES = 128
VMEM_LIMIT_BYTES = 56 * 1024 * 1024

F32 = jnp.float32
BF16 = jnp.bfloat16
HIGHEST = lax.Precision.HIGHEST

_NT = (((1,), (1,)), ((), ()))
_TN = (((0,), (0,)), ((), ()))
_TT = (((0,), (1,)), ((), ()))


def _dot(a, b, precision=None):
    return jnp.dot(a, b, preferred_element_type=F32, precision=precision)


def _dot_nt(a, b, precision=None):
    return lax.dot_general(a, b, _NT, preferred_element_type=F32, precision=precision)


def _dot_tn(a, b):
    return lax.dot_general(a, b, _TN, preferred_element_type=F32)


def _split_bf16(x, n):
    parts = []
    for _ in range(n - 1):
        p = x.astype(BF16)
        parts.append(p)
        x = x - p.astype(F32)
    parts.append(x.astype(BF16))
    return parts


def _sigmoid(x):
    return 1.0 / (1.0 + jnp.exp(-x))


def _params(*semantics):
    return pltpu.CompilerParams(dimension_semantics=semantics, vmem_limit_bytes=VMEM_LIMIT_BYTES)


def _resident(shape, index_map):
    return pl.BlockSpec(shape, index_map, pipeline_mode=pl.Buffered(1))


D_MERGE = 2
ALR_PAD = LANES


def _proj_layout(d_model):
    f32_cols = dict(gates=(0, D_MERGE * d_model))
    off = D_MERGE * d_model
    for name, w in (("ra", GLA_V_W), ("qa", GLA_QK_W), ("ka", GLA_QK_W), ("alr", ALR_PAD)):
        f32_cols[name] = (off, w)
        off += w
    wf = off
    bf_cols = {}
    off = 0
    for name, w in (("va", GLA_V_W), ("kb", MOBA_W)):
        bf_cols[name] = (off, w)
        off += w
    return f32_cols, wf, bf_cols, off


def _split_w_in(w_in, d_model, f32_names, bf16_names):
    splits = (GLA_QK_W, GLA_QK_W, GLA_V_W, GLA_V_W, GLA_RANK, MOBA_W, MOBA_W, MOBA_W, D_MERGE * d_model)
    pts = np.cumsum(splits)[:-1].tolist()
    pieces = dict(zip(("qa", "ka", "va", "ra", "alr", "qb", "kb", "vb", "gates"), jnp.split(w_in, pts, axis=-1)))
    pieces["alr"] = jnp.pad(pieces["alr"], ((0, 0), (0, 0), (0, ALR_PAD - GLA_RANK)))
    rows = [pieces[k].astype(BF16) for k in tuple(f32_names) + tuple(bf16_names)]
    transposed = [pieces[k].astype(BF16) for k in ("qb", "vb")]
    return rows, transposed


def _in_proj_kernel(x_ref, g_ref, *refs, n_f32, n_bf16, chunk):
    w_f32 = refs[:n_f32]
    w_bf16 = refs[n_f32:n_f32 + n_bf16]
    wqt_ref, wvt_ref, of_ref, ob_ref, qt_ref, vt_ref = refs[n_f32 + n_bf16:]
    x = x_ref[...]
    ms = jnp.mean(x * x, axis=-1, keepdims=True)
    h = (x * lax.rsqrt(ms + EPS) * g_ref[...]).astype(BF16)
    for w_refs, out_ref in ((w_f32, of_ref), (w_bf16, ob_ref)):
        off = 0
        for w_ref in w_refs:
            width = w_ref.shape[1]
            for c0 in range(0, width, chunk):
                c1 = min(c0 + chunk, width)
                out_ref[:, off + c0:off + c1] = _dot(h, w_ref[:, c0:c1]).astype(out_ref.dtype)
            off += width
    blk = MOBA_BLOCK
    for s in range(x.shape[0] // blk):
        hs = h[s * blk:(s + 1) * blk, :]
        qt_ref[s] = lax.dot_general(wqt_ref[...], hs, _TT, preferred_element_type=F32).astype(BF16)
        vt_ref[s] = lax.dot_general(wvt_ref[...], hs, _TT, preferred_element_type=F32).astype(BF16)


def _in_proj(x2, g, w_rows, w_t, layer, *, seq, n_f32, wf, wb, tm):
    n, d = x2.shape
    blk = MOBA_BLOCK
    tpb = seq // tm
    spt = tm // blk
    kern = functools.partial(_in_proj_kernel, n_f32=n_f32, n_bf16=len(w_rows) - n_f32, chunk=512)
    t_shape = jax.ShapeDtypeStruct((n // seq, seq // blk, MOBA_W, blk), BF16)
    t_spec = pl.BlockSpec((None, spt, MOBA_W, blk), lambda i: (i // tpb, i % tpb, 0, 0))
    w_specs = [_resident((None,) + w.shape[1:], lambda i: (layer, 0, 0)) for w in tuple(w_rows) + tuple(w_t)]
    return pl.pallas_call(
        kern,
        grid=(n // tm,),
        in_specs=[
            pl.BlockSpec((tm, d), lambda i: (i, 0)),
            pl.BlockSpec((None, 1, d), lambda i: (layer, 0, 0)),
            *w_specs,
        ],
        out_specs=[
            pl.BlockSpec((tm, wf), lambda i: (i, 0)),
            pl.BlockSpec((tm, wb), lambda i: (i, 0)),
            t_spec,
            t_spec,
        ],
        out_shape=[jax.ShapeDtypeStruct((n, wf), F32), jax.ShapeDtypeStruct((n, wb), BF16), t_shape, t_shape],
        compiler_params=_params("parallel"),
        name="in_proj",
    )(x2, g, *w_rows, *w_t)


GLA_SUPER = 256
GLA_PARTS = 4


def _unpack_ride(rest, n_ride):
    out = rest[n_ride]
    for src, dst in zip(rest[:n_ride], rest[n_ride + 1:2 * n_ride + 1]):
        dst[...] = src[...].astype(BF16)
    return out, rest[2 * n_ride + 1:]


def _ride_specs(rides, steps, step_index):
    specs, shapes = [], []
    for ride in rides:
        rows, cols = ride.shape
        assert rows % steps == 0 and (rows // steps) % 16 == 0
        specs.append(pl.BlockSpec((rows // steps, cols), lambda *g: (step_index(*g), 0)))
        shapes.append(jax.ShapeDtypeStruct((rows, cols), BF16))
    return specs, shapes


def _gla_kernel(q_ref, k_ref, v_ref, r_ref, alr_ref, wlr_ref, bf_ref, gn_ref, *rest, seq, ride):
    o_ref, (qd_ref, u_ref, dec_ref, oi_ref, sb_ref, b_ref, kd_ref, ke_ref, at_ref, la_ref) = _unpack_ride(rest, ride)
    c, sb = GLA_CHUNK, GLA_SUPER
    cps = sb // c
    nsb = seq // sb
    nchunks = seq // c
    shift = c.bit_length() - 1
    assert (1 << shift) == c and seq % sb == 0
    row = lax.broadcasted_iota(jnp.int32, (sb, sb), 0)
    col = lax.broadcasted_iota(jnp.int32, (sb, sb), 1)
    same_chunk = (row >> shift) == (col >> shift)
    causal = same_chunk & (row >= col)
    tril = causal.astype(BF16)
    wlr_parts = _split_bf16(wlr_ref[...], 2)
    bfg = bf_ref[...]
    gn = gn_ref[...]
    q_scale = GLA_DK ** -0.5

    blocks = [slice(s * sb, (s + 1) * sb) for s in range(nsb)]

    for rows in blocks:
        a_hi, a_lo = _split_bf16(alr_ref[rows, :], 2)
        xa = _dot(a_hi, wlr_parts[0]) + _dot(a_hi, wlr_parts[1]) + _dot(a_lo, wlr_parts[0]) + bfg
        log_a = (jnp.minimum(xa, 0.0) - jnp.log1p(jnp.exp(-jnp.abs(xa)))) * (1.0 / GLA_TAU)
        la_ref[rows, :] = jnp.concatenate(_split_bf16(log_a, 3), axis=1)

    for rows in blocks:
        pieces = _dot(tril, la_ref[rows, :])
        b_ref[rows, :] = (pieces[:, 0:GLA_DK] + pieces[:, GLA_DK:2 * GLA_DK]) + pieces[:, 2 * GLA_DK:3 * GLA_DK]

    for s, rows in enumerate(blocks):
        b = b_ref[rows, :]
        b_last = jnp.concatenate(
            [jnp.broadcast_to(b[(cc + 1) * c - 1:(cc + 1) * c, :], (c, GLA_DK)) for cc in range(cps)], axis=0)
        k = k_ref[rows, :]
        qd_ref[rows, :] = ((q_ref[rows, :] * q_scale) * jnp.exp(b)).astype(BF16)
        kd_ref[rows, :] = (k * jnp.exp(-b)).astype(BF16)
        ke_ref[rows, :] = (k * jnp.exp(b_last - b)).astype(BF16)
        chunk_decay = jnp.exp(b_last)
        for cc in range(cps):
            dec_ref[s * cps + cc] = chunk_decay[cc * c:cc * c + 8, :]

    def intra(part):
        for rows in part:
            at_ref[rows, :] = jnp.where(causal, _dot_nt(qd_ref[rows, :], kd_ref[rows, :]), 0.0).astype(BF16)
        for rows in part:
            for n in range(rows.start // c, rows.stop // c):
                u_ref[n] = _dot_tn(v_ref[n * c:(n + 1) * c, :], ke_ref[n * c:(n + 1) * c, :])
        for rows in part:
            oi_ref[rows, :] = _dot(at_ref[rows, :], v_ref[rows, :])

    def scan(part, st):
        for n in range(part[0].start // c, part[-1].stop // c):
            sb_ref[n] = st.astype(BF16)
            st = st * dec_ref[n, 0:1, :] + u_ref[n]
        return st

    def finish(part):
        for rows in part:
            for n in range(rows.start // c, rows.stop // c):
                crow = slice(n * c, (n + 1) * c)
                oi_ref[crow, :] = oi_ref[crow, :] + _dot_nt(qd_ref[crow, :], sb_ref[n])
        for rows in part:
            o = oi_ref[rows, :]
            ms = jnp.mean(o * o, axis=-1, keepdims=True)
            r = r_ref[rows, :]
            out = (o * lax.rsqrt(ms + EPS) * gn) * (r * _sigmoid(r))
            o_ref[rows, :] = out.astype(BF16)

    per_part = nsb // GLA_PARTS
    parts = [blocks[p * per_part:(p + 1) * per_part] for p in range(GLA_PARTS)]
    state = jnp.zeros((GLA_DV, GLA_DK), F32)
    intra(parts[0])
    for p in range(GLA_PARTS):
        state = scan(parts[p], state)
        if p + 1 < GLA_PARTS:
            intra(parts[p + 1])
        finish(parts[p])


def _gla(pf3, pb3, wlr, bforget, gnorm, layer, *, fcols, bcols, ride=()):
    bsz, seq, _ = pf3.shape
    qa0, ka0, ra0, alr0 = (fcols[k][0] for k in ("qa", "ka", "ra", "alr"))
    va0 = bcols["va"][0]
    kern = functools.partial(_gla_kernel, seq=seq, ride=len(ride))
    ride_in, ride_out_shape = _ride_specs(ride, bsz * GLA_HEADS, lambda b, h: b * GLA_HEADS + h)
    ride_out_spec, ride_args = ride_in, list(ride)
    outs = pl.pallas_call(
        kern,
        grid=(bsz, GLA_HEADS),
        in_specs=[
            pl.BlockSpec((None, seq, GLA_DK), lambda b, h: (b, 0, qa0 // GLA_DK + h)),
            pl.BlockSpec((None, seq, GLA_DK), lambda b, h: (b, 0, ka0 // GLA_DK + h)),
            pl.BlockSpec((None, seq, GLA_DV), lambda b, h: (b, 0, va0 // GLA_DV + h)),
            pl.BlockSpec((None, seq, GLA_DV), lambda b, h: (b, 0, ra0 // GLA_DV + h)),
            pl.BlockSpec((None, seq, ALR_PAD), lambda b, h: (b, 0, alr0 // ALR_PAD)),
            pl.BlockSpec((None, ALR_PAD, GLA_DK), lambda b, h: (layer, 0, h)),
            pl.BlockSpec((None, 1, GLA_DK), lambda b, h: (layer, 0, h)),
            pl.BlockSpec((None, 1, GLA_DV), lambda b, h: (layer, 0, h)),
        ] + ride_in,
        out_specs=[pl.BlockSpec((None, seq, GLA_DV), lambda b, h: (b, 0, h))] + ride_out_spec,
        out_shape=[jax.ShapeDtypeStruct((bsz, seq, GLA_V_W), BF16)] + ride_out_shape,
        scratch_shapes=[
            pltpu.VMEM((seq, GLA_DK), BF16),
            pltpu.VMEM((seq // GLA_CHUNK, GLA_DV, GLA_DK), F32),
            pltpu.VMEM((seq // GLA_CHUNK, 8, GLA_DK), F32),
            pltpu.VMEM((seq, GLA_DV), F32),
            pltpu.VMEM((seq // GLA_CHUNK, GLA_DV, GLA_DK), BF16),
            pltpu.VMEM((seq, GLA_DK), F32),
            pltpu.VMEM((seq, GLA_DK), BF16),
            pltpu.VMEM((seq, GLA_DK), BF16),
            pltpu.VMEM((seq, GLA_SUPER), BF16),
            pltpu.VMEM((seq, 3 * GLA_DK), BF16),
        ],
        compiler_params=_params("parallel", "parallel"),
        name="gla",
    )(pf3, pf3, pb3, pf3, pf3, wlr, bforget, gnorm, *ride_args)
    return outs[0], outs[1:]


def _t5_bucket(rel):
    n = jnp.maximum(rel, 0)
    max_exact = REL_BUCKETS // 2
    nf = jnp.maximum(n, 1).astype(F32)
    large = max_exact + (jnp.log(nf / max_exact) / math.log(REL_MAX_DIST / max_exact)
                         * (REL_BUCKETS - max_exact)).astype(jnp.int32)
    large = jnp.minimum(large, REL_BUCKETS - 1)
    return jnp.where(n < max_exact, n, large)


def _far_bucket(seq):
    d = np.arange(MOBA_BLOCK + 1, max(seq, MOBA_BLOCK + 2), dtype=np.float64)
    max_exact = REL_BUCKETS // 2
    large = max_exact + np.floor(np.log(d / max_exact) / math.log(REL_MAX_DIST / max_exact)
                                 * (REL_BUCKETS - max_exact) + 1e-6)
    assert MOBA_BLOCK + 1 >= max_exact and np.all(large >= REL_BUCKETS - 1), "far blocks must share one bucket"
    return REL_BUCKETS - 1


def _bias_kernel(bucket_ref, rb_ref, o_ref, *, far_bucket):
    h = pl.program_id(0)
    blk = MOBA_BLOCK
    far = rb_ref[far_bucket, h]
    row = lax.broadcasted_iota(jnp.int32, (blk, blk), 0)
    col = lax.broadcasted_iota(jnp.int32, (blk, blk), 1)
    for t in range(2):
        bk = bucket_ref[t]
        acc = jnp.zeros((blk, blk), F32)
        for b in range(REL_BUCKETS):
            acc = jnp.where(bk == b, rb_ref[b, h], acc)
        acc = acc - far
        if t == 1:
            acc = jnp.where(col >= row, acc, NEG_INF)
        o_ref[t] = acc


def _bias_tiles(rel_bias, seq):
    blk = MOBA_BLOCK
    kpos = jnp.arange(blk, dtype=jnp.int32)[:, None]
    qpos = jnp.arange(blk, dtype=jnp.int32)[None, :]
    bucket = jnp.stack([_t5_bucket(qpos + blk - kpos), _t5_bucket(qpos - kpos)])
    kern = functools.partial(_bias_kernel, far_bucket=_far_bucket(seq))
    return pl.pallas_call(
        kern,
        grid=(MOBA_HEADS,),
        in_specs=[
            pl.BlockSpec((2, blk, blk), lambda h: (0, 0, 0)),
            pl.BlockSpec(memory_space=pltpu.SMEM),
        ],
        out_specs=pl.BlockSpec((None, 2, blk, blk), lambda h: (h, 0, 0, 0)),
        out_shape=jax.ShapeDtypeStruct((MOBA_HEADS, 2, blk, blk), F32),
        compiler_params=_params("arbitrary"),
        name="rel_bias_tiles",
    )(bucket, rel_bias)


PEN_ROWS = 16
DENOM_ROWS = 16


def _moba_kernel(qt_ref, k_ref, vt_ref, bias_ref, *rest, nb, ride):
    o_ref, (kaug_ref, kmean_ref, qaug_ref, acc_ref, s_ref) = _unpack_ride(rest, ride)
    i = pl.program_id(1)
    blk, dh, heads = MOBA_BLOCK, MOBA_DH, MOBA_HEADS
    seq = nb * blk
    pair_w = 2 * dh
    assert pair_w == LANES and nb <= PEN_ROWS <= dh
    blk_shift = blk.bit_length() - 1
    assert (1 << blk_shift) == blk
    lane = lax.broadcasted_iota(jnp.int32, (1, pair_w), 1)

    @pl.when(i == 0)
    def _build_keys():
        rowblk = lax.broadcasted_iota(jnp.int32, (seq, pair_w), 0) >> blk_shift
        lane_s = lax.broadcasted_iota(jnp.int32, (seq, pair_w), 1)
        for p in range(heads // 2):
            kp = k_ref[:, p * pair_w:(p + 1) * pair_w]
            km = jnp.sum(kp.astype(F32).reshape(nb, blk, pair_w), axis=1) * (1.0 / blk)
            kmean_ref[p] = km
            for e in range(2):
                own = (lane_s >= dh) if e else (lane_s < dh)
                onehot = (lane_s - (0 if e else dh)) == rowblk
                kaug_ref[2 * p + e] = jnp.where(own, kp, onehot.astype(BF16))

    row8 = lax.broadcasted_iota(jnp.int32, (nb, blk), 0)
    scale = jnp.asarray(dh ** -0.5, BF16)

    gates = []
    for p in range(heads // 2):
        qtp = qt_ref[p * pair_w:(p + 1) * pair_w, :]
        km = kmean_ref[p]
        for e in range(2):
            own = (lane >= dh) if e else (lane < dh)
            km_hi, km_lo = _split_bf16(jnp.where(own, km, 0.0), 2)
            gates.append(_dot(km_hi, qtp) + _dot(km_lo, qtp))
    for p in range(heads // 2):
        qtp = qt_ref[p * pair_w:(p + 1) * pair_w, :]
        for e in range(2):
            g = jnp.where(row8 < i, gates[2 * p + e], -jnp.inf)
            rank = jnp.zeros((nb, blk), jnp.int32)
            for m in range(nb):
                gm = g[m:m + 1, :]
                beats = (gm > g) | ((gm == g) & (m < row8))
                rank = rank + beats.astype(jnp.int32)
            keep = ((rank < MOBA_TOPK) & (row8 < i)) | (row8 == i)
            pen_t = jnp.where(keep, 0.0, NEG_INF)
            pen = jnp.concatenate([pen_t, jnp.zeros((PEN_ROWS - nb, blk), F32)], axis=0).astype(BF16)
            qs = qtp[e * dh:(e + 1) * dh, :] * scale
            if e == 0:
                parts = [qs, pen, jnp.zeros((pair_w - dh - PEN_ROWS, blk), BF16)]
            else:
                parts = [pen, jnp.zeros((dh - PEN_ROWS, blk), BF16), qs]
            qaug_ref[2 * p + e] = jnp.concatenate(parts, axis=0)

    def scores(j, nblocks, slot):
        col_max = []
        for h in range(heads):
            s = _dot(kaug_ref[h, j * blk:(j + 1) * blk, :], qaug_ref[h])
            if j >= nblocks - 2:
                s = s + bias_ref[h, j - (nblocks - 2)]
            s_ref[slot, h] = s
            col_max.append(jnp.max(s, axis=0, keepdims=True))
        return col_max

    ones_rows = jnp.ones((DENOM_ROWS, blk), BF16)

    def attend(nblocks):
        col_max = scores(0, nblocks, 0)
        ms = [None] * heads
        for j in range(nblocks):
            next_max = scores(j + 1, nblocks, (j + 1) % 2) if j + 1 < nblocks else None
            for h in range(heads):
                m_new = col_max[h] if j == 0 else jnp.maximum(ms[h], col_max[h])
                pr = jnp.exp(s_ref[j % 2, h] - m_new).astype(BF16)
                vt_ones = jnp.concatenate([vt_ref[j, h * dh:(h + 1) * dh, :], ones_rows], axis=0)
                pv = _dot(vt_ones, pr)
                if j == 0:
                    acc_ref[h] = pv
                else:
                    acc_ref[h] = jnp.exp(ms[h] - m_new) * acc_ref[h] + pv
                ms[h] = m_new
            col_max = next_max
        for h in range(heads):
            o_ref[h * dh:(h + 1) * dh, :] = (acc_ref[h, 0:dh, :] / acc_ref[h, dh:dh + 1, :]).astype(BF16)

    for nblocks in range(1, nb + 1):
        pl.when(i == nblocks - 1)(functools.partial(attend, nblocks))


def _moba(qt, pb3, vt, bias, *, bcols, ride=()):
    bsz, seq, _ = pb3.shape
    blk = MOBA_BLOCK
    nb = seq // blk
    assert seq % blk == 0
    kb0 = bcols["kb"][0]
    kern = functools.partial(_moba_kernel, nb=nb, ride=len(ride))
    ride_in, ride_out_shape = _ride_specs(ride, bsz * nb, lambda b, i: b * nb + i)
    ride_out_spec, ride_args = ride_in, list(ride)
    outs = pl.pallas_call(
        kern,
        grid=(bsz, nb),
        in_specs=[
            pl.BlockSpec((None, None, MOBA_W, blk), lambda b, i: (b, i, 0, 0)),
            pl.BlockSpec((None, seq, MOBA_W), lambda b, i: (b, 0, kb0 // MOBA_W)),
            pl.BlockSpec((None, nb, MOBA_W, blk), lambda b, i: (b, 0, 0, 0)),
            _resident((MOBA_HEADS, 2, blk, blk), lambda b, i: (0, 0, 0, 0)),
        ] + ride_in,
        out_specs=[pl.BlockSpec((None, None, MOBA_W, blk), lambda b, i: (b, i, 0, 0))] + ride_out_spec,
        out_shape=[jax.ShapeDtypeStruct((bsz, nb, MOBA_W, blk), BF16)] + ride_out_shape,
        scratch_shapes=[
            pltpu.VMEM((MOBA_HEADS, seq, 2 * MOBA_DH), BF16),
            pltpu.VMEM((MOBA_HEADS // 2, nb, 2 * MOBA_DH), F32),
            pltpu.VMEM((MOBA_HEADS, 2 * MOBA_DH, blk), BF16),
            pltpu.VMEM((MOBA_HEADS, MOBA_DH + DENOM_ROWS, blk), F32),
            pltpu.VMEM((2, MOBA_HEADS, blk, blk), F32),
        ],
        compiler_params=_params("arbitrary", "arbitrary"),
        name="moba",
    )(qt, pb3, vt, bias, *ride_args)
    return outs[0], outs[1:]


HALO = 8


def _mix_ffn_kernel(oa_ref, obt_ref, gate_ref, x_ref, wg_ref, wm_ref, wo_ref, g_ref, wu_ref, cw_ref, cb_ref,
                    wd_ref, gf_ref, o_ref, carry_ref, act_ref, *, d_ff, cw, kgroup, final):
    t = pl.program_id(1)
    tm, d = x_ref.shape

    @pl.when(t == 0)
    def _zero_history():
        carry_ref[...] = jnp.zeros_like(carry_ref)

    ya = _dot(oa_ref[...], wg_ref[...])
    wm = wm_ref[...]
    yb = jnp.concatenate([_dot_tn(obt_ref[s], wm) for s in range(obt_ref.shape[0])], axis=0)
    mixed = _sigmoid(gate_ref[:, 0:d]) * ya + _sigmoid(gate_ref[:, d:2 * d]) * yb
    x = x_ref[...] + _dot(mixed.astype(BF16), wo_ref[...])

    ms = jnp.mean(x * x, axis=-1, keepdims=True)
    h = (x * lax.rsqrt(ms + EPS) * g_ref[...]).astype(BF16)

    row_h = lax.broadcasted_iota(jnp.int32, (HALO, cw), 0)

    def shifted(u, prev, s):
        rolled = pltpu.roll(u, s, axis=0)
        head = jnp.where(row_h < s, pltpu.roll(prev, s, axis=0), rolled[0:HALO, :])
        return jnp.concatenate([head, rolled[HALO:, :]], axis=0)

    y = x
    nchunks = d_ff // cw
    lag = 1
    flushed = 0
    for c in range(nchunks):
        halves = []
        for part in range(2):
            c0 = part * d_ff + c * cw
            halves.append(_dot(h, wu_ref[:, c0:c0 + cw]))
        done = c - lag
        if done >= 0 and ((done + 1) % kgroup == 0):
            k0 = (done + 1 - kgroup) * cw
            k1 = (done + 1) * cw
            y = y + _dot(act_ref[:, k0:k1], wd_ref[k0:k1, :])
            flushed = done + 1
        for part in range(2):
            c0 = part * d_ff + c * cw
            u = halves[part]
            prev = carry_ref[:, c0:c0 + cw]
            carry_ref[:, c0:c0 + cw] = u[tm - HALO:tm, :]
            w = cw_ref[:, c0:c0 + cw]
            conv = cb_ref[:, c0:c0 + cw] + w[CONV_W - 1:CONV_W, :] * u
            for s in range(1, CONV_W):
                conv = conv + w[CONV_W - 1 - s:CONV_W - s, :] * shifted(u, prev, s)
            halves[part] = conv
        a, bval = halves
        act_ref[:, c * cw:(c + 1) * cw] = ((a * _sigmoid(a)) * bval).astype(BF16)
    if flushed < nchunks:
        y = y + _dot(act_ref[:, flushed * cw:d_ff], wd_ref[flushed * cw:d_ff, :])
    if final:
        ms2 = jnp.mean(y * y, axis=-1, keepdims=True)
        y = y * lax.rsqrt(ms2 + EPS) * gf_ref[...]
    o_ref[...] = y


def _mix_ffn(oa3, obt, pf3, x3, wg, wm, wo, g, wu, cw_, cb, wd, gfinal, layer, *, tm, final):
    bsz, seq, d = x3.shape
    d_ff = wd.shape[1]
    blk = MOBA_BLOCK
    cw = 256
    assert d_ff % cw == 0 and seq % tm == 0 and tm % blk == 0
    kern = functools.partial(_mix_ffn_kernel, d_ff=d_ff, cw=cw, kgroup=4, final=final)
    return pl.pallas_call(
        kern,
        grid=(bsz, seq // tm),
        in_specs=[
            pl.BlockSpec((None, tm, GLA_V_W), lambda b, t: (b, t, 0)),
            pl.BlockSpec((None, tm // blk, MOBA_W, blk), lambda b, t: (b, t, 0, 0)),
            pl.BlockSpec((None, tm, D_MERGE * d), lambda b, t: (b, t, 0)),
            pl.BlockSpec((None, tm, d), lambda b, t: (b, t, 0)),
            _resident((None, GLA_V_W, d), lambda b, t: (layer, 0, 0)),
            _resident((None, MOBA_W, d), lambda b, t: (layer, 0, 0)),
            _resident((None, d, d), lambda b, t: (layer, 0, 0)),
            pl.BlockSpec((None, 1, d), lambda b, t: (layer, 0, 0)),
            _resident((None, d, 2 * d_ff), lambda b, t: (layer, 0, 0)),
            pl.BlockSpec((None, CONV_W, 2 * d_ff), lambda b, t: (layer, 0, 0)),
            pl.BlockSpec((None, 1, 2 * d_ff), lambda b, t: (layer, 0, 0)),
            _resident((None, d_ff, d), lambda b, t: (layer, 0, 0)),
            pl.BlockSpec((1, d), lambda b, t: (0, 0)),
        ],
        out_specs=pl.BlockSpec((None, tm, d), lambda b, t: (b, t, 0)),
        out_shape=jax.ShapeDtypeStruct((bsz, seq, d), F32),
        scratch_shapes=[
            pltpu.VMEM((HALO, 2 * d_ff), F32),
            pltpu.VMEM((tm, d_ff), BF16),
        ],
        compiler_params=_params("arbitrary", "arbitrary"),
        name="mix_ffn",
    )(oa3, obt, pf3, x3, wg, wm, wo, g, wu, cw_, cb, wd, gfinal)


def kernel(x, rel_bias, norm_mix, w_in, w_lr_up, b_forget, gla_out_norm, w_branch_gla, w_branch_moba,
           w_out, norm_ffn, w_up, conv_w, conv_b, w_down, norm_final):
    bsz, seq, d = x.shape
    depth = w_in.shape[0]
    n = bsz * seq
    fcols, wf, bcols, wb = _proj_layout(d)

    w_in_rows, w_in_t = _split_w_in(w_in, d, fcols, bcols)
    wlr = jnp.pad(w_lr_up, ((0, 0), (0, ALR_PAD - GLA_RANK), (0, 0)))
    norm_mix3 = norm_mix[:, None, :]
    norm_ffn3 = norm_ffn[:, None, :]
    b_forget3 = b_forget[:, None, :]
    gnorm3 = gla_out_norm[:, None, :]
    conv_b3 = conv_b[:, None, :]
    gfinal = norm_final[None, :]

    bias = _bias_tiles(rel_bias, seq)

    for l in range(depth):
        pf, pb, qt, vt = _in_proj(x.reshape(n, d), norm_mix3, w_in_rows, w_in_t, l,
                                  seq=seq, n_f32=len(fcols), wf=wf, wb=wb, tm=512)
        pf3 = pf.reshape(bsz, seq, wf)
        pb3 = pb.reshape(bsz, seq, wb)
        gla_ride = (w_up,) if l == 0 else ()
        moba_ride = (w_down, w_branch_gla, w_branch_moba, w_out) if l == 0 else ()
        oa, gla_cast = _gla(pf3, pb3, wlr, b_forget3, gnorm3, l, fcols=fcols, bcols=bcols,
                            ride=[w.reshape(-1, w.shape[-1]) for w in gla_ride])
        obt, moba_cast = _moba(qt, pb3, vt, bias, bcols=bcols,
                               ride=[w.reshape(-1, w.shape[-1]) for w in moba_ride])
        if l == 0:
            (wu,) = (c.reshape(w.shape) for c, w in zip(gla_cast, gla_ride))
            wd, wg, wm, wo = (c.reshape(w.shape) for c, w in zip(moba_cast, moba_ride))
        assert fcols["gates"][0] == 0
        x = _mix_ffn(oa, obt, pf3, x, wg, wm, wo, norm_ffn3, wu, conv_w, conv_b3, wd, gfinal, l,
                     tm=512, final=(l == depth - 1))
    return x
```

```python
import functools
import math

import jax
import jax.numpy as jnp
import numpy as np
from jax import lax
from jax.experimental import pallas as pl
from jax.experimental.pallas import tpu as pltpu

GLA_HEADS = 4
GLA_DK = 128
GLA_DV = 256
GLA_RANK = 16
GLA_TAU = 16.0
GLA_CHUNK = 64
MOBA_HEADS = 8
MOBA_DH = 64
MOBA_BLOCK = 256
MOBA_TOPK = 3
REL_BUCKETS = 32
REL_MAX_DIST = 128
CONV_W = 3
EPS = 1e-6
NEG_INF = -1e30

GLA_QK_W = GLA_HEADS * GLA_DK
GLA_V_W = GLA_HEADS * GLA_DV
MOBA_W = MOBA_HEADS * MOBA_DH

LANES = 128
VMEM_LIMIT_BYTES = 56 * 1024 * 1024

F32 = jnp.float32
BF16 = jnp.bfloat16
HIGHEST = lax.Precision.HIGHEST

_NT = (((1,), (1,)), ((), ()))
_TN = (((0,), (0,)), ((), ()))
_TT = (((0,), (1,)), ((), ()))


def _dot(a, b, precision=None):
    return jnp.dot(a, b, preferred_element_type=F32, precision=precision)


def _dot_nt(a, b, precision=None):
    return lax.dot_general(a, b, _NT, preferred_element_type=F32, precision=precision)


def _dot_tn(a, b):
    return lax.dot_general(a, b, _TN, preferred_element_type=F32)


def _split_bf16(x, n):
    parts = []
    for _ in range(n - 1):
        p = x.astype(BF16)
        parts.append(p)
        x = x - p.astype(F32)
    parts.append(x.astype(BF16))
    return parts


def _sigmoid(x):
    return 1.0 / (1.0 + jnp.exp(-x))


def _params(*semantics):
    return pltpu.CompilerParams(dimension_semantics=semantics, vmem_limit_bytes=VMEM_LIMIT_BYTES)


def _resident(shape, index_map):
    return pl.BlockSpec(shape, index_map, pipeline_mode=pl.Buffered(1))


D_MERGE = 2
ALR_PAD = LANES


def _proj_layout(d_model):
    f32_cols = dict(gates=(0, D_MERGE * d_model))
    off = D_MERGE * d_model
    for name, w in (("ra", GLA_V_W), ("qa", GLA_QK_W), ("ka", GLA_QK_W), ("alr", ALR_PAD)):
        f32_cols[name] = (off, w)
        off += w
    wf = off
    bf_cols = {}
    off = 0
    for name, w in (("va", GLA_V_W), ("kb", MOBA_W)):
        bf_cols[name] = (off, w)
        off += w
    return f32_cols, wf, bf_cols, off


def _split_w_in(w_in, d_model, f32_names, bf16_names):
    splits = (GLA_QK_W, GLA_QK_W, GLA_V_W, GLA_V_W, GLA_RANK, MOBA_W, MOBA_W, MOBA_W, D_MERGE * d_model)
    pts = np.cumsum(splits)[:-1].tolist()
    w_bf16 = w_in.astype(BF16)
    pieces = dict(zip(("qa", "ka", "va", "ra", "alr", "qb", "kb", "vb", "gates"), jnp.split(w_bf16, pts, axis=-1)))
    pieces["alr"] = jnp.pad(pieces["alr"], ((0, 0), (0, 0), (0, ALR_PAD - GLA_RANK)))
    rows = [pieces[k] for k in tuple(f32_names) + tuple(bf16_names)]
    transposed = [pieces[k] for k in ("qb", "vb")]
    return rows, transposed


def _in_proj_kernel(x_ref, g_ref, *refs, n_f32, n_bf16, chunk):
    w_f32 = refs[:n_f32]
    w_bf16 = refs[n_f32:n_f32 + n_bf16]
    wqt_ref, wvt_ref, of_ref, ob_ref, qt_ref, vt_ref = refs[n_f32 + n_bf16:]
    x = x_ref[...]
    ms = jnp.mean(x * x, axis=-1, keepdims=True)
    h = (x * lax.rsqrt(ms + EPS) * g_ref[...]).astype(BF16)
    for w_refs, out_ref in ((w_f32, of_ref), (w_bf16, ob_ref)):
        off = 0
        for w_ref in w_refs:
            width = w_ref.shape[1]
            for c0 in range(0, width, chunk):
                c1 = min(c0 + chunk, width)
                out_ref[:, off + c0:off + c1] = _dot(h, w_ref[:, c0:c1]).astype(out_ref.dtype)
            off += width
    blk = MOBA_BLOCK
    for s in range(x.shape[0] // blk):
        hs = h[s * blk:(s + 1) * blk, :]
        qt_ref[s] = lax.dot_general(wqt_ref[...], hs, _TT, preferred_element_type=F32).astype(BF16)
        vt_ref[s] = lax.dot_general(wvt_ref[...], hs, _TT, preferred_element_type=F32).astype(BF16)


def _in_proj(x2, g, w_rows, w_t, layer, *, seq, n_f32, wf, wb, tm):
    n, d = x2.shape
    blk = MOBA_BLOCK
    tpb = seq // tm
    spt = tm // blk
    kern = functools.partial(_in_proj_kernel, n_f32=n_f32, n_bf16=len(w_rows) - n_f32, chunk=512)
    t_shape = jax.ShapeDtypeStruct((n // seq, seq // blk, MOBA_W, blk), BF16)
    t_spec = pl.BlockSpec((None, spt, MOBA_W, blk), lambda i: (i // tpb, i % tpb, 0, 0))
    w_specs = [_resident((None,) + w.shape[1:], lambda i: (layer, 0, 0)) for w in tuple(w_rows) + tuple(w_t)]
    return pl.pallas_call(
        kern,
        grid=(n // tm,),
        in_specs=[
            pl.BlockSpec((tm, d), lambda i: (i, 0)),
            pl.BlockSpec((None, 1, d), lambda i: (layer, 0, 0)),
            *w_specs,
        ],
        out_specs=[
            pl.BlockSpec((tm, wf), lambda i: (i, 0)),
            pl.BlockSpec((tm, wb), lambda i: (i, 0)),
            t_spec,
            t_spec,
        ],
        out_shape=[jax.ShapeDtypeStruct((n, wf), F32), jax.ShapeDtypeStruct((n, wb), BF16), t_shape, t_shape],
        compiler_params=_params("parallel"),
        name="in_proj",
    )(x2, g, *w_rows, *w_t)


GLA_SUPER = 256
GLA_PARTS = 4


def _unpack_ride(rest, n_ride):
    out = rest[n_ride]
    for src, dst in zip(rest[:n_ride], rest[n_ride + 1:2 * n_ride + 1]):
        dst[...] = src[...].astype(BF16)
    return out, rest[2 * n_ride + 1:]


def _ride_specs(rides, steps, step_index):
    specs, shapes = [], []
    for ride in rides:
        rows, cols = ride.shape
        assert rows % steps == 0 and (rows // steps) % 16 == 0
        specs.append(pl.BlockSpec((rows // steps, cols), lambda *g: (step_index(*g), 0)))
        shapes.append(jax.ShapeDtypeStruct((rows, cols), BF16))
    return specs, shapes


def _gla_kernel(q_ref, k_ref, v_ref, r_ref, alr_ref, wlr_ref, bf_ref, gn_ref, *rest, seq, ride):
    o_ref, (qd_ref, u_ref, dec_ref, oi_ref, sb_ref, b_ref, kd_ref, ke_ref, at_ref, la_ref) = _unpack_ride(rest, ride)
    c, sb = GLA_CHUNK, GLA_SUPER
    cps = sb // c
    nsb = seq // sb
    nchunks = seq // c
    shift = c.bit_length() - 1
    assert (1 << shift) == c and seq % sb == 0
    row = lax.broadcasted_iota(jnp.int32, (sb, sb), 0)
    col = lax.broadcasted_iota(jnp.int32, (sb, sb), 1)
    same_chunk = (row >> shift) == (col >> shift)
    causal = same_chunk & (row >= col)
    tril = causal.astype(BF16)
    wlr_parts = _split_bf16(wlr_ref[...], 2)
    bfg = bf_ref[...]
    gn = gn_ref[...]
    q_scale = GLA_DK ** -0.5

    blocks = [slice(s * sb, (s + 1) * sb) for s in range(nsb)]

    for rows in blocks:
        a_hi, a_lo = _split_bf16(alr_ref[rows, :], 2)
        xa = _dot(a_hi, wlr_parts[0]) + _dot(a_hi, wlr_parts[1]) + _dot(a_lo, wlr_parts[0]) + bfg
        log_a = (jnp.minimum(xa, 0.0) - jnp.log1p(jnp.exp(-jnp.abs(xa)))) * (1.0 / GLA_TAU)
        la_ref[rows, :] = jnp.concatenate(_split_bf16(log_a, 3), axis=1)

    for rows in blocks:
        pieces = _dot(tril, la_ref[rows, :])
        b_ref[rows, :] = (pieces[:, 0:GLA_DK] + pieces[:, GLA_DK:2 * GLA_DK]) + pieces[:, 2 * GLA_DK:3 * GLA_DK]

    for s, rows in enumerate(blocks):
        b = b_ref[rows, :]
        b_last = jnp.concatenate(
            [jnp.broadcast_to(b[(cc + 1) * c - 1:(cc + 1) * c, :], (c, GLA_DK)) for cc in range(cps)], axis=0)
        k = k_ref[rows, :]
        qd_ref[rows, :] = ((q_ref[rows, :] * q_scale) * jnp.exp(b)).astype(BF16)
        kd_ref[rows, :] = (k * jnp.exp(-b)).astype(BF16)
        ke_ref[rows, :] = (k * jnp.exp(b_last - b)).astype(BF16)
        chunk_decay = jnp.exp(b_last)
        for cc in range(cps):
            dec_ref[s * cps + cc] = chunk_decay[cc * c:cc * c + 8, :]

    def intra(part):
        for rows in part:
            at_ref[rows, :] = jnp.where(causal, _dot_nt(qd_ref[rows, :], kd_ref[rows, :]), 0.0).astype(BF16)
        for rows in part:
            for n in range(rows.start // c, rows.stop // c):
                u_ref[n] = _dot_tn(v_ref[n * c:(n + 1) * c, :], ke_ref[n * c:(n + 1) * c, :])
        for rows in part:
            oi_ref[rows, :] = _dot(at_ref[rows, :], v_ref[rows, :])

    def scan(part, st):
        for n in range(part[0].start // c, part[-1].stop // c):
            sb_ref[n] = st.astype(BF16)
            st = st * dec_ref[n, 0:1, :] + u_ref[n]
        return st

    def finish(part):
        for rows in part:
            for n in range(rows.start // c, rows.stop // c):
                crow = slice(n * c, (n + 1) * c)
                oi_ref[crow, :] = oi_ref[crow, :] + _dot_nt(qd_ref[crow, :], sb_ref[n])
        for rows in part:
            o = oi_ref[rows, :]
            ms = jnp.mean(o * o, axis=-1, keepdims=True)
            r = r_ref[rows, :]
            out = (o * lax.rsqrt(ms + EPS) * gn) * (r * _sigmoid(r))
            o_ref[rows, :] = out.astype(BF16)

    per_part = nsb // GLA_PARTS
    parts = [blocks[p * per_part:(p + 1) * per_part] for p in range(GLA_PARTS)]
    state = jnp.zeros((GLA_DV, GLA_DK), F32)
    intra(parts[0])
    for p in range(GLA_PARTS):
        state = scan(parts[p], state)
        if p + 1 < GLA_PARTS:
            intra(parts[p + 1])
        finish(parts[p])


def _gla(pf3, pb3, wlr, bforget, gnorm, layer, *, fcols, bcols, ride=()):
    bsz, seq, _ = pf3.shape
    qa0, ka0, ra0, alr0 = (fcols[k][0] for k in ("qa", "ka", "ra", "alr"))
    va0 = bcols["va"][0]
    kern = functools.partial(_gla_kernel, seq=seq, ride=len(ride))
    ride_in, ride_out_shape = _ride_specs(ride, bsz * GLA_HEADS, lambda b, h: b * GLA_HEADS + h)
    ride_out_spec, ride_args = ride_in, list(ride)
    outs = pl.pallas_call(
        kern,
        grid=(bsz, GLA_HEADS),
        in_specs=[
            pl.BlockSpec((None, seq, GLA_DK), lambda b, h: (b, 0, qa0 // GLA_DK + h)),
            pl.BlockSpec((None, seq, GLA_DK), lambda b, h: (b, 0, ka0 // GLA_DK + h)),
            pl.BlockSpec((None, seq, GLA_DV), lambda b, h: (b, 0, va0 // GLA_DV + h)),
            pl.BlockSpec((None, seq, GLA_DV), lambda b, h: (b, 0, ra0 // GLA_DV + h)),
            pl.BlockSpec((None, seq, ALR_PAD), lambda b, h: (b, 0, alr0 // ALR_PAD)),
            pl.BlockSpec((None, ALR_PAD, GLA_DK), lambda b, h: (layer, 0, h)),
            pl.BlockSpec((None, 1, GLA_DK), lambda b, h: (layer, 0, h)),
            pl.BlockSpec((None, 1, GLA_DV), lambda b, h: (layer, 0, h)),
        ] + ride_in,
        out_specs=[pl.BlockSpec((None, seq, GLA_DV), lambda b, h: (b, 0, h))] + ride_out_spec,
        out_shape=[jax.ShapeDtypeStruct((bsz, seq, GLA_V_W), BF16)] + ride_out_shape,
        scratch_shapes=[
            pltpu.VMEM((seq, GLA_DK), BF16),
            pltpu.VMEM((seq // GLA_CHUNK, GLA_DV, GLA_DK), F32),
            pltpu.VMEM((seq // GLA_CHUNK, 8, GLA_DK), F32),
            pltpu.VMEM((seq, GLA_DV), F32),
            pltpu.VMEM((seq // GLA_CHUNK, GLA_DV, GLA_DK), BF16),
            pltpu.VMEM((seq, GLA_DK), F32),
            pltpu.VMEM((seq, GLA_DK), BF16),
            pltpu.VMEM((seq, GLA_DK), BF16),
            pltpu.VMEM((seq, GLA_SUPER), BF16),
            pltpu.VMEM((seq, 3 * GLA_DK), BF16),
        ],
        compiler_params=_params("parallel", "parallel"),
        name="gla",
    )(pf3, pf3, pb3, pf3, pf3, wlr, bforget, gnorm, *ride_args)
    return outs[0], outs[1:]


def _t5_bucket(rel):
    n = jnp.maximum(rel, 0)
    max_exact = REL_BUCKETS // 2
    nf = jnp.maximum(n, 1).astype(F32)
    large = max_exact + (jnp.log(nf / max_exact) / math.log(REL_MAX_DIST / max_exact)
                         * (REL_BUCKETS - max_exact)).astype(jnp.int32)
    large = jnp.minimum(large, REL_BUCKETS - 1)
    return jnp.where(n < max_exact, n, large)


def _far_bucket(seq):
    d = np.arange(MOBA_BLOCK + 1, max(seq, MOBA_BLOCK + 2), dtype=np.float64)
    max_exact = REL_BUCKETS // 2
    large = max_exact + np.floor(np.log(d / max_exact) / math.log(REL_MAX_DIST / max_exact)
                                 * (REL_BUCKETS - max_exact) + 1e-6)
    assert MOBA_BLOCK + 1 >= max_exact and np.all(large >= REL_BUCKETS - 1), "far blocks must share one bucket"
    return REL_BUCKETS - 1


def _bias_kernel(bucket_ref, rb_ref, o_ref, *, far_bucket):
    h = pl.program_id(0)
    blk = MOBA_BLOCK
    far = rb_ref[far_bucket, h]
    row = lax.broadcasted_iota(jnp.int32, (blk, blk), 0)
    col = lax.broadcasted_iota(jnp.int32, (blk, blk), 1)
    for t in range(2):
        bk = bucket_ref[t]
        acc = jnp.zeros((blk, blk), F32)
        for b in range(REL_BUCKETS):
            acc = jnp.where(bk == b, rb_ref[b, h], acc)
        acc = acc - far
        if t == 1:
            acc = jnp.where(col >= row, acc, NEG_INF)
        o_ref[t] = acc


def _bias_tiles(rel_bias, seq):
    blk = MOBA_BLOCK
    kpos = jnp.arange(blk, dtype=jnp.int32)[:, None]
    qpos = jnp.arange(blk, dtype=jnp.int32)[None, :]
    bucket = jnp.stack([_t5_bucket(qpos + blk - kpos), _t5_bucket(qpos - kpos)])
    kern = functools.partial(_bias_kernel, far_bucket=_far_bucket(seq))
    return pl.pallas_call(
        kern,
        grid=(MOBA_HEADS,),
        in_specs=[
            pl.BlockSpec((2, blk, blk), lambda h: (0, 0, 0)),
            pl.BlockSpec(memory_space=pltpu.SMEM),
        ],
        out_specs=pl.BlockSpec((None, 2, blk, blk), lambda h: (h, 0, 0, 0)),
        out_shape=jax.ShapeDtypeStruct((MOBA_HEADS, 2, blk, blk), F32),
        compiler_params=_params("arbitrary"),
        name="rel_bias_tiles",
    )(bucket, rel_bias)


PEN_ROWS = 16
DENOM_ROWS = 16


def _moba_kernel(qt_ref, k_ref, vt_ref, bias_ref, *rest, nb, ride):
    o_ref, (kaug_ref, kmean_ref, qaug_ref, acc_ref, s_ref) = _unpack_ride(rest, ride)
    i = pl.program_id(1)
    blk, dh, heads = MOBA_BLOCK, MOBA_DH, MOBA_HEADS
    seq = nb * blk
    pair_w = 2 * dh
    assert pair_w == LANES and nb <= PEN_ROWS <= dh
    blk_shift = blk.bit_length() - 1
    assert (1 << blk_shift) == blk
    lane = lax.broadcasted_iota(jnp.int32, (1, pair_w), 1)

    @pl.when(i == 0)
    def _build_keys():
        rowblk = lax.broadcasted_iota(jnp.int32, (seq, pair_w), 0) >> blk_shift
        lane_s = lax.broadcasted_iota(jnp.int32, (seq, pair_w), 1)
        for p in range(heads // 2):
            kp = k_ref[:, p * pair_w:(p + 1) * pair_w]
            km = jnp.sum(kp.astype(F32).reshape(nb, blk, pair_w), axis=1) * (1.0 / blk)
            kmean_ref[p] = km
            for e in range(2):
                own = (lane_s >= dh) if e else (lane_s < dh)
                onehot = (lane_s - (0 if e else dh)) == rowblk
                kaug_ref[2 * p + e] = jnp.where(own, kp, onehot.astype(BF16))

    row8 = lax.broadcasted_iota(jnp.int32, (nb, blk), 0)
    scale = jnp.asarray(dh ** -0.5, BF16)

    gates = []
    for p in range(heads // 2):
        qtp = qt_ref[p * pair_w:(p + 1) * pair_w, :]
        km = kmean_ref[p]
        for e in range(2):
            own = (lane >= dh) if e else (lane < dh)
            km_hi, km_lo = _split_bf16(jnp.where(own, km, 0.0), 2)
            gates.append(_dot(km_hi, qtp) + _dot(km_lo, qtp))
    for p in range(heads // 2):
        qtp = qt_ref[p * pair_w:(p + 1) * pair_w, :]
        for e in range(2):
            g = jnp.where(row8 < i, gates[2 * p + e], -jnp.inf)
            rank = jnp.zeros((nb, blk), jnp.int32)
            for m in range(nb):
                gm = g[m:m + 1, :]
                beats = (gm > g) | ((gm == g) & (m < row8))
                rank = rank + beats.astype(jnp.int32)
            keep = ((rank < MOBA_TOPK) & (row8 < i)) | (row8 == i)
            pen_t = jnp.where(keep, 0.0, NEG_INF)
            pen = jnp.concatenate([pen_t, jnp.zeros((PEN_ROWS - nb, blk), F32)], axis=0).astype(BF16)
            qs = qtp[e * dh:(e + 1) * dh, :] * scale
            if e == 0:
                parts = [qs, pen, jnp.zeros((pair_w - dh - PEN_ROWS, blk), BF16)]
            else:
                parts = [pen, jnp.zeros((dh - PEN_ROWS, blk), BF16), qs]
            qaug_ref[2 * p + e] = jnp.concatenate(parts, axis=0)

    def scores(j, nblocks, slot):
        col_max = []
        for h in range(heads):
            s = _dot(kaug_ref[h, j * blk:(j + 1) * blk, :], qaug_ref[h])
            if j >= nblocks - 2:
                s = s + bias_ref[h, j - (nblocks - 2)]
            s_ref[slot, h] = s
            col_max.append(jnp.max(s, axis=0, keepdims=True))
        return col_max

    ones_rows = jnp.ones((DENOM_ROWS, blk), BF16)

    def attend(nblocks):
        col_max = scores(0, nblocks, 0)
        ms = [None] * heads
        for j in range(nblocks):
            next_max = scores(j + 1, nblocks, (j + 1) % 2) if j + 1 < nblocks else None
            for h in range(heads):
                m_new = col_max[h] if j == 0 else jnp.maximum(ms[h], col_max[h])
                pr = jnp.exp(s_ref[j % 2, h] - m_new).astype(BF16)
                vt_ones = jnp.concatenate([vt_ref[j, h * dh:(h + 1) * dh, :], ones_rows], axis=0)
                pv = _dot(vt_ones, pr)
                if j == 0:
                    acc_ref[h] = pv
                else:
                    acc_ref[h] = jnp.exp(ms[h] - m_new) * acc_ref[h] + pv
                ms[h] = m_new
            col_max = next_max
        for h in range(heads):
            o_ref[h * dh:(h + 1) * dh, :] = (acc_ref[h, 0:dh, :] / acc_ref[h, dh:dh + 1, :]).astype(BF16)

    for nblocks in range(1, nb + 1):
        pl.when(i == nblocks - 1)(functools.partial(attend, nblocks))


def _moba(qt, pb3, vt, bias, *, bcols, ride=()):
    bsz, seq, _ = pb3.shape
    blk = MOBA_BLOCK
    nb = seq // blk
    assert seq % blk == 0
    kb0 = bcols["kb"][0]
    kern = functools.partial(_moba_kernel, nb=nb, ride=len(ride))
    ride_in, ride_out_shape = _ride_specs(ride, bsz * nb, lambda b, i: b * nb + i)
    ride_out_spec, ride_args = ride_in, list(ride)
    outs = pl.pallas_call(
        kern,
        grid=(bsz, nb),
        in_specs=[
            pl.BlockSpec((None, None, MOBA_W, blk), lambda b, i: (b, i, 0, 0)),
            pl.BlockSpec((None, seq, MOBA_W), lambda b, i: (b, 0, kb0 // MOBA_W)),
            pl.BlockSpec((None, nb, MOBA_W, blk), lambda b, i: (b, 0, 0, 0)),
            _resident((MOBA_HEADS, 2, blk, blk), lambda b, i: (0, 0, 0, 0)),
        ] + ride_in,
        out_specs=[pl.BlockSpec((None, None, MOBA_W, blk), lambda b, i: (b, i, 0, 0))] + ride_out_spec,
        out_shape=[jax.ShapeDtypeStruct((bsz, nb, MOBA_W, blk), BF16)] + ride_out_shape,
        scratch_shapes=[
            pltpu.VMEM((MOBA_HEADS, seq, 2 * MOBA_DH), BF16),
            pltpu.VMEM((MOBA_HEADS // 2, nb, 2 * MOBA_DH), F32),
            pltpu.VMEM((MOBA_HEADS, 2 * MOBA_DH, blk), BF16),
            pltpu.VMEM((MOBA_HEADS, MOBA_DH + DENOM_ROWS, blk), F32),
            pltpu.VMEM((2, MOBA_HEADS, blk, blk), F32),
        ],
        compiler_params=_params("arbitrary", "arbitrary"),
        name="moba",
    )(qt, pb3, vt, bias, *ride_args)
    return outs[0], outs[1:]


HALO = 8


def _mix_ffn_kernel(oa_ref, obt_ref, gate_ref, x_ref, wg_ref, wm_ref, wo_ref, g_ref, wu_ref, cw_ref, cb_ref,
                    wd_ref, gf_ref, o_ref, carry_ref, act_ref, *, d_ff, cw, kgroup, final):
    t = pl.program_id(1)
    tm, d = x_ref.shape

    @pl.when(t == 0)
    def _zero_history():
        carry_ref[...] = jnp.zeros_like(carry_ref)

    ya = _dot(oa_ref[...], wg_ref[...])
    wm = wm_ref[...]
    yb = jnp.concatenate([_dot_tn(obt_ref[s], wm) for s in range(obt_ref.shape[0])], axis=0)
    mixed = _sigmoid(gate_ref[:, 0:d]) * ya + _sigmoid(gate_ref[:, d:2 * d]) * yb
    x = x_ref[...] + _dot(mixed.astype(BF16), wo_ref[...])

    ms = jnp.mean(x * x, axis=-1, keepdims=True)
    h = (x * lax.rsqrt(ms + EPS) * g_ref[...]).astype(BF16)

    row_h = lax.broadcasted_iota(jnp.int32, (HALO, cw), 0)

    def shifted(u, prev, s):
        rolled = pltpu.roll(u, s, axis=0)
        head = jnp.where(row_h < s, pltpu.roll(prev, s, axis=0), rolled[0:HALO, :])
        return jnp.concatenate([head, rolled[HALO:, :]], axis=0)

    y = x
    nchunks = d_ff // cw
    lag = 1
    flushed = 0
    for c in range(nchunks):
        halves = []
        for part in range(2):
            c0 = part * d_ff + c * cw
            halves.append(_dot(h, wu_ref[:, c0:c0 + cw]))
        done = c - lag
        if done >= 0 and ((done + 1) % kgroup == 0):
            k0 = (done + 1 - kgroup) * cw
            k1 = (done + 1) * cw
            y = y + _dot(act_ref[:, k0:k1], wd_ref[k0:k1, :])
            flushed = done + 1
        for part in range(2):
            c0 = part * d_ff + c * cw
            u = halves[part]
            prev = carry_ref[:, c0:c0 + cw]
            carry_ref[:, c0:c0 + cw] = u[tm - HALO:tm, :]
            w = cw_ref[:, c0:c0 + cw]
            conv = cb_ref[:, c0:c0 + cw] + w[CONV_W - 1:CONV_W, :] * u
            for s in range(1, CONV_W):
                conv = conv + w[CONV_W - 1 - s:CONV_W - s, :] * shifted(u, prev, s)
            halves[part] = conv
        a, bval = halves
        act_ref[:, c * cw:(c + 1) * cw] = ((a * _sigmoid(a)) * bval).astype(BF16)
    if flushed < nchunks:
        y = y + _dot(act_ref[:, flushed * cw:d_ff], wd_ref[flushed * cw:d_ff, :])
    if final:
        ms2 = jnp.mean(y * y, axis=-1, keepdims=True)
        y = y * lax.rsqrt(ms2 + EPS) * gf_ref[...]
    o_ref[...] = y


def _mix_ffn(oa3, obt, pf3, x3, wg, wm, wo, g, wu, cw_, cb, wd, gfinal, layer, *, tm, final):
    bsz, seq, d = x3.shape
    d_ff = wd.shape[1]
    blk = MOBA_BLOCK
    cw = 256
    assert d_ff % cw == 0 and seq % tm == 0 and tm % blk == 0
    kern = functools.partial(_mix_ffn_kernel, d_ff=d_ff, cw=cw, kgroup=4, final=final)
    return pl.pallas_call(
        kern,
        grid=(bsz, seq // tm),
        in_specs=[
            pl.BlockSpec((None, tm, GLA_V_W), lambda b, t: (b, t, 0)),
            pl.BlockSpec((None, tm // blk, MOBA_W, blk), lambda b, t: (b, t, 0, 0)),
            pl.BlockSpec((None, tm, D_MERGE * d), lambda b, t: (b, t, 0)),
            pl.BlockSpec((None, tm, d), lambda b, t: (b, t, 0)),
            _resident((None, GLA_V_W, d), lambda b, t: (layer, 0, 0)),
            _resident((None, MOBA_W, d), lambda b, t: (layer, 0, 0)),
            _resident((None, d, d), lambda b, t: (layer, 0, 0)),
            pl.BlockSpec((None, 1, d), lambda b, t: (layer, 0, 0)),
            _resident((None, d, 2 * d_ff), lambda b, t: (layer, 0, 0)),
            pl.BlockSpec((None, CONV_W, 2 * d_ff), lambda b, t: (layer, 0, 0)),
            pl.BlockSpec((None, 1, 2 * d_ff), lambda b, t: (layer, 0, 0)),
            _resident((None, d_ff, d), lambda b, t: (layer, 0, 0)),
            pl.BlockSpec((1, d), lambda b, t: (0, 0)),
        ],
        out_specs=pl.BlockSpec((None, tm, d), lambda b, t: (b, t, 0)),
        out_shape=jax.ShapeDtypeStruct((bsz, seq, d), F32),
        scratch_shapes=[
            pltpu.VMEM((HALO, 2 * d_ff), F32),
            pltpu.VMEM((tm, d_ff), BF16),
        ],
        compiler_params=_params("arbitrary", "arbitrary"),
        name="mix_ffn",
    )(oa3, obt, pf3, x3, wg, wm, wo, g, wu, cw_, cb, wd, gfinal)


def kernel(x, rel_bias, norm_mix, w_in, w_lr_up, b_forget, gla_out_norm, w_branch_gla, w_branch_moba,
           w_out, norm_ffn, w_up, conv_w, conv_b, w_down, norm_final):
    bsz, seq, d = x.shape
    depth = w_in.shape[0]
    n = bsz * seq
    fcols, wf, bcols, wb = _proj_layout(d)

    w_in_rows, w_in_t = _split_w_in(w_in, d, fcols, bcols)
    wlr = jnp.pad(w_lr_up, ((0, 0), (0, ALR_PAD - GLA_RANK), (0, 0)))
    norm_mix3 = norm_mix[:, None, :]
    norm_ffn3 = norm_ffn[:, None, :]
    b_forget3 = b_forget[:, None, :]
    gnorm3 = gla_out_norm[:, None, :]
    conv_b3 = conv_b[:, None, :]
    gfinal = norm_final[None, :]

    bias = _bias_tiles(rel_bias, seq)

    for l in range(depth):
        pf, pb, qt, vt = _in_proj(x.reshape(n, d), norm_mix3, w_in_rows, w_in_t, l,
                                  seq=seq, n_f32=len(fcols), wf=wf, wb=wb, tm=512)
        pf3 = pf.reshape(bsz, seq, wf)
        pb3 = pb.reshape(bsz, seq, wb)
        gla_ride = (w_up,) if l == 0 else ()
        moba_ride = (w_down, w_branch_gla, w_branch_moba, w_out) if l == 0 else ()
        oa, gla_cast = _gla(pf3, pb3, wlr, b_forget3, gnorm3, l, fcols=fcols, bcols=bcols,
                            ride=[w.reshape(-1, w.shape[-1]) for w in gla_ride])
        obt, moba_cast = _moba(qt, pb3, vt, bias, bcols=bcols,
                               ride=[w.reshape(-1, w.shape[-1]) for w in moba_ride])
        if l == 0:
            (wu,) = (c.reshape(w.shape) for c, w in zip(gla_cast, gla_ride))
            wd, wg, wm, wo = (c.reshape(w.shape) for c, w in zip(moba_cast, moba_ride))
        assert fcols["gates"][0] == 0
        x = _mix_ffn(oa, obt, pf3, x, wg, wm, wo, norm_ffn3, wu, conv_w, conv_b3, wd, gfinal, l,
                     tm=512, final=(l == depth - 1))
    return x
```

```python
import functools
import math

import jax
import jax.numpy as jnp
import numpy as np
from jax import lax
from jax.experimental import pallas as pl
from jax.experimental.pallas import tpu as pltpu

GLA_HEADS = 4
GLA_DK = 128
GLA_DV = 256
GLA_RANK = 16
GLA_TAU = 16.0
GLA_CHUNK = 64
MOBA_HEADS = 8
MOBA_DH = 64
MOBA_BLOCK = 256
MOBA_TOPK = 3
REL_BUCKETS = 32
REL_MAX_DIST = 128
CONV_W = 3
EPS = 1e-6
NEG_INF = -1e30

GLA_QK_W = GLA_HEADS * GLA_DK
GLA_V_W = GLA_HEADS * GLA_DV
MOBA_W = MOBA_HEADS * MOBA_DH

LANES = 128
VMEM_LIMIT_BYTES = 56 * 1024 * 1024

F32 = jnp.float32
BF16 = jnp.bfloat16
HIGHEST = lax.Precision.HIGHEST

_NT = (((1,), (1,)), ((), ()))
_TN = (((0,), (0,)), ((), ()))
_TT = (((0,), (1,)), ((), ()))


def _dot(a, b, precision=None):
    return jnp.dot(a, b, preferred_element_type=F32, precision=precision)


def _dot_nt(a, b, precision=None):
    return lax.dot_general(a, b, _NT, preferred_element_type=F32, precision=precision)


def _dot_tn(a, b):
    return lax.dot_general(a, b, _TN, preferred_element_type=F32)


def _split_bf16(x, n):
    parts = []
    for _ in range(n - 1):
        p = x.astype(BF16)
        parts.append(p)
        x = x - p.astype(F32)
    parts.append(x.astype(BF16))
    return parts


def _sigmoid(x):
    return 1.0 / (1.0 + jnp.exp(-x))


def _params(*semantics):
    return pltpu.CompilerParams(dimension_semantics=semantics, vmem_limit_bytes=VMEM_LIMIT_BYTES)


def _resident(shape, index_map):
    return pl.BlockSpec(shape, index_map, pipeline_mode=pl.Buffered(1))


D_MERGE = 2
ALR_PAD = LANES


def _proj_layout(d_model):
    f32_cols = dict(gates=(0, D_MERGE * d_model))
    off = D_MERGE * d_model
    for name, w in (("ra", GLA_V_W), ("qa", GLA_QK_W), ("ka", GLA_QK_W), ("alr", ALR_PAD)):
        f32_cols[name] = (off, w)
        off += w
    wf = off
    bf_cols = {}
    off = 0
    for name, w in (("va", GLA_V_W), ("kb", MOBA_W)):
        bf_cols[name] = (off, w)
        off += w
    return f32_cols, wf, bf_cols, off


def _split_w_in(w_in, d_model, f32_names, bf16_names):
    splits = (GLA_QK_W, GLA_QK_W, GLA_V_W, GLA_V_W, GLA_RANK, MOBA_W, MOBA_W, MOBA_W, D_MERGE * d_model)
    pts = np.cumsum(splits)[:-1].tolist()
    w_bf16 = w_in.astype(BF16)
    pieces = dict(zip(("qa", "ka", "va", "ra", "alr", "qb", "kb", "vb", "gates"), jnp.split(w_bf16, pts, axis=-1)))
    pieces["alr"] = jnp.pad(pieces["alr"], ((0, 0), (0, 0), (0, ALR_PAD - GLA_RANK)))
    rows = [pieces[k] for k in tuple(f32_names) + tuple(bf16_names)]
    transposed = [pieces[k] for k in ("qb", "vb")]
    return rows, transposed


def _in_proj_kernel(x_ref, g_ref, *refs, n_f32, n_bf16, chunk):
    w_f32 = refs[:n_f32]
    w_bf16 = refs[n_f32:n_f32 + n_bf16]
    wqt_ref, wvt_ref, of_ref, ob_ref, qt_ref, vt_ref = refs[n_f32 + n_bf16:]
    x = x_ref[...]
    ms = jnp.mean(x * x, axis=-1, keepdims=True)
    h = (x * lax.rsqrt(ms + EPS) * g_ref[...]).astype(BF16)
    for w_refs, out_ref in ((w_f32, of_ref), (w_bf16, ob_ref)):
        off = 0
        for w_ref in w_refs:
            width = w_ref.shape[1]
            for c0 in range(0, width, chunk):
                c1 = min(c0 + chunk, width)
                out_ref[:, off + c0:off + c1] = _dot(h, w_ref[:, c0:c1]).astype(out_ref.dtype)
            off += width
    blk = MOBA_BLOCK
    for s in range(x.shape[0] // blk):
        hs = h[s * blk:(s + 1) * blk, :]
        qt_ref[s] = lax.dot_general(wqt_ref[...], hs, _TT, preferred_element_type=F32).astype(BF16)
        vt_ref[s] = lax.dot_general(wvt_ref[...], hs, _TT, preferred_element_type=F32).astype(BF16)


def _in_proj(x2, g, w_rows, w_t, layer, *, seq, n_f32, wf, wb, tm):
    n, d = x2.shape
    blk = MOBA_BLOCK
    tpb = seq // tm
    spt = tm // blk
    kern = functools.partial(_in_proj_kernel, n_f32=n_f32, n_bf16=len(w_rows) - n_f32, chunk=512)
    t_shape = jax.ShapeDtypeStruct((n // seq, seq // blk, MOBA_W, blk), BF16)
    t_spec = pl.BlockSpec((None, spt, MOBA_W, blk), lambda i: (i // tpb, i % tpb, 0, 0))
    w_specs = [_resident((None,) + w.shape[1:], lambda i: (layer, 0, 0)) for w in tuple(w_rows) + tuple(w_t)]
    return pl.pallas_call(
        kern,
        grid=(n // tm,),
        in_specs=[
            pl.BlockSpec((tm, d), lambda i: (i, 0)),
            pl.BlockSpec((None, 1, d), lambda i: (layer, 0, 0)),
            *w_specs,
        ],
        out_specs=[
            pl.BlockSpec((tm, wf), lambda i: (i, 0)),
            pl.BlockSpec((tm, wb), lambda i: (i, 0)),
            t_spec,
            t_spec,
        ],
        out_shape=[jax.ShapeDtypeStruct((n, wf), F32), jax.ShapeDtypeStruct((n, wb), BF16), t_shape, t_shape],
        compiler_params=_params("parallel"),
        name="in_proj",
    )(x2, g, *w_rows, *w_t)


GLA_SUPER = 256
GLA_PARTS = 4


def _unpack_ride(rest, n_ride):
    out = rest[n_ride]
    for src, dst in zip(rest[:n_ride], rest[n_ride + 1:2 * n_ride + 1]):
        dst[...] = src[...].astype(BF16)
    return out, rest[2 * n_ride + 1:]


def _ride_specs(rides, steps, step_index):
    specs, shapes = [], []
    for ride in rides:
        rows, cols = ride.shape
        assert rows % steps == 0 and (rows // steps) % 16 == 0
        specs.append(pl.BlockSpec((rows // steps, cols), lambda *g: (step_index(*g), 0)))
        shapes.append(jax.ShapeDtypeStruct((rows, cols), BF16))
    return specs, shapes


def _gla_kernel(q_ref, k_ref, v_ref, r_ref, alr_ref, wlr_ref, bf_ref, gn_ref, *rest, seq, ride):
    o_ref, (qd_ref, u_ref, dec_ref, oi_ref, sb_ref, b_ref, kd_ref, ke_ref, at_ref, la_ref) = _unpack_ride(rest, ride)
    c, sb = GLA_CHUNK, GLA_SUPER
    cps = sb // c
    nsb = seq // sb
    nchunks = seq // c
    shift = c.bit_length() - 1
    assert (1 << shift) == c and seq % sb == 0
    row = lax.broadcasted_iota(jnp.int32, (sb, sb), 0)
    col = lax.broadcasted_iota(jnp.int32, (sb, sb), 1)
    same_chunk = (row >> shift) == (col >> shift)
    causal = same_chunk & (row >= col)
    tril = causal.astype(BF16)
    wlr_parts = _split_bf16(wlr_ref[...], 2)
    bfg = bf_ref[...]
    gn = gn_ref[...]
    q_scale = GLA_DK ** -0.5

    blocks = [slice(s * sb, (s + 1) * sb) for s in range(nsb)]

    for rows in blocks:
        a_hi, a_lo = _split_bf16(alr_ref[rows, :], 2)
        xa = _dot(a_hi, wlr_parts[0]) + _dot(a_hi, wlr_parts[1]) + _dot(a_lo, wlr_parts[0]) + bfg
        log_a = (jnp.minimum(xa, 0.0) - jnp.log1p(jnp.exp(-jnp.abs(xa)))) * (1.0 / GLA_TAU)
        la_ref[rows, :] = jnp.concatenate(_split_bf16(log_a, 3), axis=1)

    for rows in blocks:
        pieces = _dot(tril, la_ref[rows, :])
        b_ref[rows, :] = (pieces[:, 0:GLA_DK] + pieces[:, GLA_DK:2 * GLA_DK]) + pieces[:, 2 * GLA_DK:3 * GLA_DK]

    for s, rows in enumerate(blocks):
        b = b_ref[rows, :]
        b_last = jnp.concatenate(
            [jnp.broadcast_to(b[(cc + 1) * c - 1:(cc + 1) * c, :], (c, GLA_DK)) for cc in range(cps)], axis=0)
        k = k_ref[rows, :]
        qd_ref[rows, :] = ((q_ref[rows, :] * q_scale) * jnp.exp(b)).astype(BF16)
        kd_ref[rows, :] = (k * jnp.exp(-b)).astype(BF16)
        ke_ref[rows, :] = (k * jnp.exp(b_last - b)).astype(BF16)
        chunk_decay = jnp.exp(b_last)
        for cc in range(cps):
            dec_ref[s * cps + cc] = chunk_decay[cc * c:cc * c + 8, :]

    def intra(part):
        for rows in part:
            at_ref[rows, :] = jnp.where(causal, _dot_nt(qd_ref[rows, :], kd_ref[rows, :]), 0.0).astype(BF16)
        for rows in part:
            for n in range(rows.start // c, rows.stop // c):
                u_ref[n] = _dot_tn(v_ref[n * c:(n + 1) * c, :], ke_ref[n * c:(n + 1) * c, :])
        for rows in part:
            oi_ref[rows, :] = _dot(at_ref[rows, :], v_ref[rows, :])

    def scan(part, st):
        for n in range(part[0].start // c, part[-1].stop // c):
            sb_ref[n] = st.astype(BF16)
            st = st * dec_ref[n, 0:1, :] + u_ref[n]
        return st

    def finish(part):
        for rows in part:
            for n in range(rows.start // c, rows.stop // c):
                crow = slice(n * c, (n + 1) * c)
                oi_ref[crow, :] = oi_ref[crow, :] + _dot_nt(qd_ref[crow, :], sb_ref[n])
        for rows in part:
            o = oi_ref[rows, :]
            ms = jnp.mean(o * o, axis=-1, keepdims=True)
            r = r_ref[rows, :]
            out = (o * lax.rsqrt(ms + EPS) * gn) * (r * _sigmoid(r))
            o_ref[rows, :] = out.astype(BF16)

    per_part = nsb // GLA_PARTS
    parts = [blocks[p * per_part:(p + 1) * per_part] for p in range(GLA_PARTS)]
    state = jnp.zeros((GLA_DV, GLA_DK), F32)
    intra(parts[0])
    for p in range(GLA_PARTS):
        state = scan(parts[p], state)
        if p + 1 < GLA_PARTS:
            intra(parts[p + 1])
        finish(parts[p])


def _gla(pf3, pb3, wlr, bforget, gnorm, layer, *, fcols, bcols, ride=()):
    bsz, seq, _ = pf3.shape
    qa0, ka0, ra0, alr0 = (fcols[k][0] for k in ("qa", "ka", "ra", "alr"))
    va0 = bcols["va"][0]
    kern = functools.partial(_gla_kernel, seq=seq, ride=len(ride))
    ride_in, ride_out_shape = _ride_specs(ride, bsz * GLA_HEADS, lambda b, h: b * GLA_HEADS + h)
    ride_out_spec, ride_args = ride_in, list(ride)
    outs = pl.pallas_call(
        kern,
        grid=(bsz, GLA_HEADS),
        in_specs=[
            pl.BlockSpec((None, seq, GLA_DK), lambda b, h: (b, 0, qa0 // GLA_DK + h)),
            pl.BlockSpec((None, seq, GLA_DK), lambda b, h: (b, 0, ka0 // GLA_DK + h)),
            pl.BlockSpec((None, seq, GLA_DV), lambda b, h: (b, 0, va0 // GLA_DV + h)),
            pl.BlockSpec((None, seq, GLA_DV), lambda b, h: (b, 0, ra0 // GLA_DV + h)),
            pl.BlockSpec((None, seq, ALR_PAD), lambda b, h: (b, 0, alr0 // ALR_PAD)),
            pl.BlockSpec((None, ALR_PAD, GLA_DK), lambda b, h: (layer, 0, h)),
            pl.BlockSpec((None, 1, GLA_DK), lambda b, h: (layer, 0, h)),
            pl.BlockSpec((None, 1, GLA_DV), lambda b, h: (layer, 0, h)),
        ] + ride_in,
        out_specs=[pl.BlockSpec((None, seq, GLA_DV), lambda b, h: (b, 0, h))] + ride_out_spec,
        out_shape=[jax.ShapeDtypeStruct((bsz, seq, GLA_V_W), BF16)] + ride_out_shape,
        scratch_shapes=[
            pltpu.VMEM((seq, GLA_DK), BF16),
            pltpu.VMEM((seq // GLA_CHUNK, GLA_DV, GLA_DK), F32),
            pltpu.VMEM((seq // GLA_CHUNK, 8, GLA_DK), F32),
            pltpu.VMEM((seq, GLA_DV), F32),
            pltpu.VMEM((seq // GLA_CHUNK, GLA_DV, GLA_DK), BF16),
            pltpu.VMEM((seq, GLA_DK), F32),
            pltpu.VMEM((seq, GLA_DK), BF16),
            pltpu.VMEM((seq, GLA_DK), BF16),
            pltpu.VMEM((seq, GLA_SUPER), BF16),
            pltpu.VMEM((seq, 3 * GLA_DK), BF16),
        ],
        compiler_params=_params("parallel", "parallel"),
        name="gla",
    )(pf3, pf3, pb3, pf3, pf3, wlr, bforget, gnorm, *ride_args)
    return outs[0], outs[1:]


def _t5_bucket(rel):
    n = jnp.maximum(rel, 0)
    max_exact = REL_BUCKETS // 2
    nf = jnp.maximum(n, 1).astype(F32)
    large = max_exact + (jnp.log(nf / max_exact) / math.log(REL_MAX_DIST / max_exact)
                         * (REL_BUCKETS - max_exact)).astype(jnp.int32)
    large = jnp.minimum(large, REL_BUCKETS - 1)
    return jnp.where(n < max_exact, n, large)


def _far_bucket(seq):
    d = np.arange(MOBA_BLOCK + 1, max(seq, MOBA_BLOCK + 2), dtype=np.float64)
    max_exact = REL_BUCKETS // 2
    large = max_exact + np.floor(np.log(d / max_exact) / math.log(REL_MAX_DIST / max_exact)
                                 * (REL_BUCKETS - max_exact) + 1e-6)
    assert MOBA_BLOCK + 1 >= max_exact and np.all(large >= REL_BUCKETS - 1), "far blocks must share one bucket"
    return REL_BUCKETS - 1


def _bias_kernel(bucket_ref, rb_ref, o_ref, *, far_bucket):
    h = pl.program_id(0)
    blk = MOBA_BLOCK
    far = rb_ref[far_bucket, h]
    row = lax.broadcasted_iota(jnp.int32, (blk, blk), 0)
    col = lax.broadcasted_iota(jnp.int32, (blk, blk), 1)
    for t in range(2):
        bk = bucket_ref[t]
        acc = jnp.zeros((blk, blk), F32)
        for b in range(REL_BUCKETS):
            acc = jnp.where(bk == b, rb_ref[b, h], acc)
        acc = acc - far
        if t == 1:
            acc = jnp.where(col >= row, acc, NEG_INF)
        o_ref[t] = acc


def _bias_tiles(rel_bias, seq):
    blk = MOBA_BLOCK
    kpos = jnp.arange(blk, dtype=jnp.int32)[:, None]
    qpos = jnp.arange(blk, dtype=jnp.int32)[None, :]
    bucket = jnp.stack([_t5_bucket(qpos + blk - kpos), _t5_bucket(qpos - kpos)])
    kern = functools.partial(_bias_kernel, far_bucket=_far_bucket(seq))
    return pl.pallas_call(
        kern,
        grid=(MOBA_HEADS,),
        in_specs=[
            pl.BlockSpec((2, blk, blk), lambda h: (0, 0, 0)),
            pl.BlockSpec(memory_space=pltpu.SMEM),
        ],
        out_specs=pl.BlockSpec((None, 2, blk, blk), lambda h: (h, 0, 0, 0)),
        out_shape=jax.ShapeDtypeStruct((MOBA_HEADS, 2, blk, blk), F32),
        compiler_params=_params("arbitrary"),
        name="rel_bias_tiles",
    )(bucket, rel_bias)


PEN_ROWS = 16
DENOM_ROWS = 16


def _moba_kernel(qt_ref, k_ref, vt_ref, bias_ref, *rest, nb, ride):
    o_ref, (kaug_ref, kmean_ref, qaug_ref, acc_ref, s_ref) = _unpack_ride(rest, ride)
    i = pl.program_id(1)
    blk, dh, heads = MOBA_BLOCK, MOBA_DH, MOBA_HEADS
    seq = nb * blk
    pair_w = 2 * dh
    assert pair_w == LANES and nb <= PEN_ROWS <= dh
    blk_shift = blk.bit_length() - 1
    assert (1 << blk_shift) == blk
    lane = lax.broadcasted_iota(jnp.int32, (1, pair_w), 1)

    @pl.when(i == 0)
    def _build_keys():
        rowblk = lax.broadcasted_iota(jnp.int32, (seq, pair_w), 0) >> blk_shift
        lane_s = lax.broadcasted_iota(jnp.int32, (seq, pair_w), 1)
        for p in range(heads // 2):
            kp = k_ref[:, p * pair_w:(p + 1) * pair_w]
            km = jnp.sum(kp.astype(F32).reshape(nb, blk, pair_w), axis=1) * (1.0 / blk)
            kmean_ref[p] = km
            for e in range(2):
                own = (lane_s >= dh) if e else (lane_s < dh)
                onehot = (lane_s - (0 if e else dh)) == rowblk
                kaug_ref[2 * p + e] = jnp.where(own, kp, onehot.astype(BF16))

    row8 = lax.broadcasted_iota(jnp.int32, (nb, blk), 0)
    scale = jnp.asarray(dh ** -0.5, BF16)

    gates = []
    for p in range(heads // 2):
        qtp = qt_ref[p * pair_w:(p + 1) * pair_w, :]
        km = kmean_ref[p]
        for e in range(2):
            own = (lane >= dh) if e else (lane < dh)
            km_hi, km_lo = _split_bf16(jnp.where(own, km, 0.0), 2)
            gates.append(_dot(km_hi, qtp) + _dot(km_lo, qtp))
    for p in range(heads // 2):
        qtp = qt_ref[p * pair_w:(p + 1) * pair_w, :]
        for e in range(2):
            g = jnp.where(row8 < i, gates[2 * p + e], -jnp.inf)
            rank = jnp.zeros((nb, blk), jnp.int32)
            for m in range(nb):
                gm = g[m:m + 1, :]
                beats = (gm > g) | ((gm == g) & (m < row8))
                rank = rank + beats.astype(jnp.int32)
            keep = ((rank < MOBA_TOPK) & (row8 < i)) | (row8 == i)
            pen_t = jnp.where(keep, 0.0, NEG_INF)
            pen = jnp.concatenate([pen_t, jnp.zeros((PEN_ROWS - nb, blk), F32)], axis=0).astype(BF16)
            qs = qtp[e * dh:(e + 1) * dh, :] * scale
            if e == 0:
                parts = [qs, pen, jnp.zeros((pair_w - dh - PEN_ROWS, blk), BF16)]
            else:
                parts = [pen, jnp.zeros((dh - PEN_ROWS, blk), BF16), qs]
            qaug_ref[2 * p + e] = jnp.concatenate(parts, axis=0)

    def scores(j, nblocks, slot):
        col_max = []
        for h in range(heads):
            s = _dot(kaug_ref[h, j * blk:(j + 1) * blk, :], qaug_ref[h])
            if j >= nblocks - 2:
                s = s + bias_ref[h, j - (nblocks - 2)]
            s_ref[slot, h] = s
            col_max.append(jnp.max(s, axis=0, keepdims=True))
        return col_max

    ones_rows = jnp.ones((DENOM_ROWS, blk), BF16)

    def attend(nblocks):
        col_max = scores(0, nblocks, 0)
        ms = [None] * heads
        for j in range(nblocks):
            next_max = scores(j + 1, nblocks, (j + 1) % 2) if j + 1 < nblocks else None
            for h in range(heads):
                m_new = col_max[h] if j == 0 else jnp.maximum(ms[h], col_max[h])
                pr = jnp.exp(s_ref[j % 2, h] - m_new).astype(BF16)
                vt_ones = jnp.concatenate([vt_ref[j, h * dh:(h + 1) * dh, :], ones_rows], axis=0)
                pv = _dot(vt_ones, pr)
                if j == 0:
                    acc_ref[h] = pv
                else:
                    acc_ref[h] = jnp.exp(ms[h] - m_new) * acc_ref[h] + pv
                ms[h] = m_new
            col_max = next_max
        for h in range(heads):
            o_ref[h * dh:(h + 1) * dh, :] = (acc_ref[h, 0:dh, :] / acc_ref[h, dh:dh + 1, :]).astype(BF16)

    for nblocks in range(1, nb + 1):
        pl.when(i == nblocks - 1)(functools.partial(attend, nblocks))


def _moba(qt, pb3, vt, bias, *, bcols, ride=()):
    bsz, seq, _ = pb3.shape
    blk = MOBA_BLOCK
    nb = seq // blk
    assert seq % blk == 0
    kb0 = bcols["kb"][0]
    kern = functools.partial(_moba_kernel, nb=nb, ride=len(ride))
    ride_in, ride_out_shape = _ride_specs(ride, bsz * nb, lambda b, i: b * nb + i)
    ride_out_spec, ride_args = ride_in, list(ride)
    outs = pl.pallas_call(
        kern,
        grid=(bsz, nb),
        in_specs=[
            pl.BlockSpec((None, None, MOBA_W, blk), lambda b, i: (b, i, 0, 0)),
            pl.BlockSpec((None, seq, MOBA_W), lambda b, i: (b, 0, kb0 // MOBA_W)),
            pl.BlockSpec((None, nb, MOBA_W, blk), lambda b, i: (b, 0, 0, 0)),
            _resident((MOBA_HEADS, 2, blk, blk), lambda b, i: (0, 0, 0, 0)),
        ] + ride_in,
        out_specs=[pl.BlockSpec((None, None, MOBA_W, blk), lambda b, i: (b, i, 0, 0))] + ride_out_spec,
        out_shape=[jax.ShapeDtypeStruct((bsz, nb, MOBA_W, blk), BF16)] + ride_out_shape,
        scratch_shapes=[
            pltpu.VMEM((MOBA_HEADS, seq, 2 * MOBA_DH), BF16),
            pltpu.VMEM((MOBA_HEADS // 2, nb, 2 * MOBA_DH), F32),
            pltpu.VMEM((MOBA_HEADS, 2 * MOBA_DH, blk), BF16),
            pltpu.VMEM((MOBA_HEADS, MOBA_DH + DENOM_ROWS, blk), F32),
            pltpu.VMEM((2, MOBA_HEADS, blk, blk), F32),
        ],
        compiler_params=_params("arbitrary", "arbitrary"),
        name="moba",
    )(qt, pb3, vt, bias, *ride_args)
    return outs[0], outs[1:]


HALO = 8


def _mix_ffn_kernel(oa_ref, obt_ref, gate_ref, x_ref, wg_ref, wm_ref, wo_ref, g_ref, wu_ref, cw_ref, cb_ref,
                    wd_ref, gf_ref, o_ref, carry_ref, act_ref, *, d_ff, cw, kgroup, final):
    t = pl.program_id(1)
    tm, d = x_ref.shape

    @pl.when(t == 0)
    def _zero_history():
        carry_ref[...] = jnp.zeros_like(carry_ref)

    ya = _dot(oa_ref[...], wg_ref[...])
    wm = wm_ref[...]
    yb = jnp.concatenate([_dot_tn(obt_ref[s], wm) for s in range(obt_ref.shape[0])], axis=0)
    mixed = _sigmoid(gate_ref[:, 0:d]) * ya + _sigmoid(gate_ref[:, d:2 * d]) * yb
    x = x_ref[...] + _dot(mixed.astype(BF16), wo_ref[...])

    ms = jnp.mean(x * x, axis=-1, keepdims=True)
    h = (x * lax.rsqrt(ms + EPS) * g_ref[...]).astype(BF16)

    row_h = lax.broadcasted_iota(jnp.int32, (HALO, cw), 0)

    def shifted(u, prev, s):
        rolled = pltpu.roll(u, s, axis=0)
        head = jnp.where(row_h < s, pltpu.roll(prev, s, axis=0), rolled[0:HALO, :])
        return jnp.concatenate([head, rolled[HALO:, :]], axis=0)

    y = x
    nchunks = d_ff // cw
    lag = 1
    flushed = 0
    for c in range(nchunks):
        halves = []
        for part in range(2):
            c0 = part * d_ff + c * cw
            halves.append(_dot(h, wu_ref[:, c0:c0 + cw]))
        done = c - lag
        if done >= 0 and ((done + 1) % kgroup == 0):
            k0 = (done + 1 - kgroup) * cw
            k1 = (done + 1) * cw
            y = y + _dot(act_ref[:, k0:k1], wd_ref[k0:k1, :])
            flushed = done + 1
        for part in range(2):
            c0 = part * d_ff + c * cw
            u = halves[part]
            prev = carry_ref[:, c0:c0 + cw]
            carry_ref[:, c0:c0 + cw] = u[tm - HALO:tm, :]
            w = cw_ref[:, c0:c0 + cw]
            conv = cb_ref[:, c0:c0 + cw] + w[CONV_W - 1:CONV_W, :] * u
            for s in range(1, CONV_W):
                conv = conv + w[CONV_W - 1 - s:CONV_W - s, :] * shifted(u, prev, s)
            halves[part] = conv
        a, bval = halves
        act_ref[:, c * cw:(c + 1) * cw] = ((a * _sigmoid(a)) * bval).astype(BF16)
    if flushed < nchunks:
        y = y + _dot(act_ref[:, flushed * cw:d_ff], wd_ref[flushed * cw:d_ff, :])
    if final:
        ms2 = jnp.mean(y * y, axis=-1, keepdims=True)
        y = y * lax.rsqrt(ms2 + EPS) * gf_ref[...]
    o_ref[...] = y


def _mix_ffn(oa3, obt, pf3, x3, wg, wm, wo, g, wu, cw_, cb, wd, gfinal, layer, *, tm, final):
    bsz, seq, d = x3.shape
    d_ff = wd.shape[1]
    blk = MOBA_BLOCK
    cw = 256
    assert d_ff % cw == 0 and seq % tm == 0 and tm % blk == 0
    kern = functools.partial(_mix_ffn_kernel, d_ff=d_ff, cw=cw, kgroup=11, final=final)
    return pl.pallas_call(
        kern,
        grid=(bsz, seq // tm),
        in_specs=[
            pl.BlockSpec((None, tm, GLA_V_W), lambda b, t: (b, t, 0)),
            pl.BlockSpec((None, tm // blk, MOBA_W, blk), lambda b, t: (b, t, 0, 0)),
            pl.BlockSpec((None, tm, D_MERGE * d), lambda b, t: (b, t, 0)),
            pl.BlockSpec((None, tm, d), lambda b, t: (b, t, 0)),
            _resident((None, GLA_V_W, d), lambda b, t: (layer, 0, 0)),
            _resident((None, MOBA_W, d), lambda b, t: (layer, 0, 0)),
            _resident((None, d, d), lambda b, t: (layer, 0, 0)),
            pl.BlockSpec((None, 1, d), lambda b, t: (layer, 0, 0)),
            _resident((None, d, 2 * d_ff), lambda b, t: (layer, 0, 0)),
            pl.BlockSpec((None, CONV_W, 2 * d_ff), lambda b, t: (layer, 0, 0)),
            pl.BlockSpec((None, 1, 2 * d_ff), lambda b, t: (layer, 0, 0)),
            _resident((None, d_ff, d), lambda b, t: (layer, 0, 0)),
            pl.BlockSpec((1, d), lambda b, t: (0, 0)),
        ],
        out_specs=pl.BlockSpec((None, tm, d), lambda b, t: (b, t, 0)),
        out_shape=jax.ShapeDtypeStruct((bsz, seq, d), F32),
        scratch_shapes=[
            pltpu.VMEM((HALO, 2 * d_ff), F32),
            pltpu.VMEM((tm, d_ff), BF16),
        ],
        compiler_params=_params("arbitrary", "arbitrary"),
        name="mix_ffn",
    )(oa3, obt, pf3, x3, wg, wm, wo, g, wu, cw_, cb, wd, gfinal)


def kernel(x, rel_bias, norm_mix, w_in, w_lr_up, b_forget, gla_out_norm, w_branch_gla, w_branch_moba,
           w_out, norm_ffn, w_up, conv_w, conv_b, w_down, norm_final):
    bsz, seq, d = x.shape
    depth = w_in.shape[0]
    n = bsz * seq
    fcols, wf, bcols, wb = _proj_layout(d)

    w_in_rows, w_in_t = _split_w_in(w_in, d, fcols, bcols)
    wlr = jnp.pad(w_lr_up, ((0, 0), (0, ALR_PAD - GLA_RANK), (0, 0)))
    norm_mix3 = norm_mix[:, None, :]
    norm_ffn3 = norm_ffn[:, None, :]
    b_forget3 = b_forget[:, None, :]
    gnorm3 = gla_out_norm[:, None, :]
    conv_b3 = conv_b[:, None, :]
    gfinal = norm_final[None, :]

    bias = _bias_tiles(rel_bias, seq)

    for l in range(depth):
        pf, pb, qt, vt = _in_proj(x.reshape(n, d), norm_mix3, w_in_rows, w_in_t, l,
                                  seq=seq, n_f32=len(fcols), wf=wf, wb=wb, tm=512)
        pf3 = pf.reshape(bsz, seq, wf)
        pb3 = pb.reshape(bsz, seq, wb)
        gla_ride = (w_up,) if l == 0 else ()
        moba_ride = (w_down, w_branch_gla, w_branch_moba, w_out) if l == 0 else ()
        oa, gla_cast = _gla(pf3, pb3, wlr, b_forget3, gnorm3, l, fcols=fcols, bcols=bcols,
                            ride=[w.reshape(-1, w.shape[-1]) for w in gla_ride])
        obt, moba_cast = _moba(qt, pb3, vt, bias, bcols=bcols,
                               ride=[w.reshape(-1, w.shape[-1]) for w in moba_ride])
        if l == 0:
            (wu,) = (c.reshape(w.shape) for c, w in zip(gla_cast, gla_ride))
            wd, wg, wm, wo = (c.reshape(w.shape) for c, w in zip(moba_cast, moba_ride))
        assert fcols["gates"][0] == 0
        x = _mix_ffn(oa, obt, pf3, x, wg, wm, wo, norm_ffn3, wu, conv_w, conv_b3, wd, gfinal, l,
                     tm=512, final=(l == depth - 1))
    return x
```
